```python
import jax, jax.numpy as jnp
from jax import lax
import numpy as np

D_MODEL = 1024
BATCH = 1
SEQ = 16384
DEPTH = 2
DEC_BATCH = 32
DEC_SEQ = 1
PAST_LEN = 16384
PAGE_SIZE = 128

N_HEADS = 8
HEAD_DIM = D_MODEL // N_HEADS
BLOCK = 256
TOP_K = 3
Q_CHUNK = 64
CONV_W = 3
D_FF = 4 * D_MODEL
N_ATTN = (DEPTH + 1) // 2
N_CONV = DEPTH // 2
EPS = 1e-6

kernel_name = 'moba_shortconv_hybrid_step'


def rmsnorm(x, g):
    x32 = x.astype(jnp.float32)
    y = x32 * lax.rsqrt(jnp.mean(x32 * x32, axis=-1, keepdims=True) + EPS)
    return (y * g.astype(jnp.float32)).astype(x.dtype)


def sq_relu_mlp(x, w_up, w_down):
    h = jax.nn.relu(x @ w_up)
    return (h * h) @ w_down


def qkv_split(h, w_qkv):
    b, l = h.shape[:2]
    qkv = (h @ w_qkv).reshape(b, l, 3, N_HEADS, HEAD_DIM)
    return qkv[:, :, 0], qkv[:, :, 1], qkv[:, :, 2]


def moba_core(q, q_pos, kmean, fetch):
    b, qn = q.shape[0], q.shape[1]
    nb = kmean.shape[1]
    n_sel = min(TOP_K, nb)
    own = q_pos // BLOCK
    gate = jnp.einsum('bqhd,bnhd->bhqn', q.astype(jnp.float32), kmean)
    eligible = jnp.arange(nb)[None, None, None, :] < own[None, None, :, None]
    gate = jnp.where(eligible, gate, -jnp.inf)
    _, sel = lax.top_k(gate, n_sel)
    own_b = jnp.broadcast_to(own[None, None, :, None], (b, N_HEADS, qn, 1)).astype(sel.dtype)
    blocks = jnp.concatenate([sel, own_b], axis=-1)
    slot_ok = jnp.concatenate([sel < own_b, jnp.ones_like(own_b, dtype=bool)], axis=-1)
    pos = blocks[..., None] * BLOCK + jnp.arange(BLOCK, dtype=blocks.dtype)
    mask = slot_ok[..., None] & (pos <= q_pos[None, None, :, None, None])
    k_rows, v_rows = fetch(pos)
    logits = jnp.einsum('bqhd,bhqsjd->bhqsj', q, k_rows).astype(jnp.float32) * (HEAD_DIM ** -0.5)
    logits = jnp.where(mask, logits, -jnp.inf)
    p = jax.nn.softmax(logits.reshape(b, N_HEADS, qn, -1), axis=-1).reshape(logits.shape)
    return jnp.einsum('bhqsj,bhqsjd->bqhd', p.astype(v_rows.dtype), v_rows)


def moba_prompt(q, k, v):
    b, s = q.shape[:2]
    nb = -(-s // BLOCK)
    pad = nb * BLOCK - s
    k_pad = jnp.pad(k, ((0, 0), (0, pad), (0, 0), (0, 0)))
    v_pad = jnp.pad(v, ((0, 0), (0, pad), (0, 0), (0, 0)))
    kmean = jnp.sum(k_pad.reshape(b, nb, BLOCK, N_HEADS, HEAD_DIM), axis=2, dtype=jnp.float32) / BLOCK
    bidx = jnp.arange(b)[:, None, None, None, None]
    hidx = jnp.arange(N_HEADS)[None, :, None, None, None]

    def fetch(pos):
        return k_pad[bidx, pos, hidx], v_pad[bidx, pos, hidx]

    nc = s // Q_CHUNK
    q_c = q.reshape(b, nc, Q_CHUNK, N_HEADS, HEAD_DIM).transpose(1, 0, 2, 3, 4)
    pos_c = jnp.arange(s, dtype=jnp.int32).reshape(nc, Q_CHUNK)
    out = lax.map(lambda a: moba_core(a[0], a[1], kmean, fetch), (q_c, pos_c))
    return out.transpose(1, 0, 2, 3, 4).reshape(b, s, N_HEADS, HEAD_DIM)


def moba_sample(q, k_new, v_new, cache_k, cache_v, page_sums_k, layer, page_table):
    db, ds = q.shape[:2]
    nb = -(-(PAST_LEN + ds) // BLOCK)
    n_pages = page_table.shape[1]
    q_pos = PAST_LEN + jnp.arange(ds, dtype=jnp.int32)
    past_sums = page_sums_k[page_table]
    page_blk = (jnp.arange(n_pages) * PAGE_SIZE) // BLOCK
    sums = (jnp.zeros((db, nb, N_HEADS, HEAD_DIM), jnp.float32)
            .at[:, page_blk].add(past_sums)
            .at[:, q_pos // BLOCK].add(k_new.astype(jnp.float32)))
    kmean = sums / BLOCK
    bidx = jnp.arange(db)[:, None, None, None, None]
    hidx = jnp.arange(N_HEADS)[None, :, None, None, None]

    def fetch(pos):
        in_past = (pos < PAST_LEN)[..., None]
        pp = jnp.clip(pos, 0, PAST_LEN - 1)
        phys = page_table[bidx, pp // PAGE_SIZE]
        row = pp % PAGE_SIZE
        pn = jnp.clip(pos - PAST_LEN, 0, ds - 1)
        k_rows = jnp.where(in_past, cache_k[layer, phys, row, hidx].astype(k_new.dtype), k_new[bidx, pn, hidx])
        v_rows = jnp.where(in_past, cache_v[layer, phys, row, hidx].astype(v_new.dtype), v_new[bidx, pn, hidx])
        return k_rows, v_rows

    return moba_core(q, q_pos, kmean, fetch)


def short_conv_mixer(h, state, w_in, conv_w, w_out):
    l = h.shape[1]
    g_b, g_c, u = jnp.split(h @ w_in, 3, axis=-1)
    z = g_c * u
    full = jnp.concatenate([state.astype(z.dtype), z], axis=1)
    y = conv_w[0] * full[:, 0:l]
    for j in range(1, CONV_W):
        y = y + conv_w[j] * full[:, j:j + l]
    return (g_b * y) @ w_out, full[:, -(CONV_W - 1):]


def setup_inputs(seed: int = 0) -> dict:
    key = jax.random.key(seed)
    ks = jax.random.split(key, 20)
    n_pages = PAST_LEN // PAGE_SIZE
    n_used = DEC_BATCH * n_pages
    n_pool = (5 * n_used + 3) // 4
    f32 = jnp.float32
    nrm = lambda k, shp, s: jax.random.normal(k, shp, f32) * s
    perm = jax.random.permutation(ks[0], n_pool)[:n_used]
    return {
        'x_prompt': nrm(ks[1], (BATCH, SEQ, D_MODEL), 1.0),
        'x_sample': nrm(ks[2], (DEC_BATCH, DEC_SEQ, D_MODEL), 1.0),
        'cache_k': nrm(ks[3], (N_ATTN, n_pool, PAGE_SIZE, N_HEADS, HEAD_DIM), 1.0),
        'cache_v': nrm(ks[4], (N_ATTN, n_pool, PAGE_SIZE, N_HEADS, HEAD_DIM), 1.0),
        'state_conv': nrm(ks[5], (N_CONV, DEC_BATCH, CONV_W - 1, D_MODEL), 1.0),
        'page_table': perm.reshape(DEC_BATCH, n_pages).astype(jnp.int32),
        'norm_mix': 1.0 + nrm(ks[6], (DEPTH, D_MODEL), 0.02),
        'norm_ffn': 1.0 + nrm(ks[7], (DEPTH, D_MODEL), 0.02),
        'norm_final': 1.0 + nrm(ks[8], (D_MODEL,), 0.02),
        'w_qkv': nrm(ks[9], (N_ATTN, D_MODEL, 3 * D_MODEL), D_MODEL ** -0.5),
        'w_o': nrm(ks[10], (N_ATTN, D_MODEL, D_MODEL), D_MODEL ** -0.5),
        'w_conv_in': nrm(ks[11], (N_CONV, D_MODEL, 3 * D_MODEL), D_MODEL ** -0.5),
        'conv_w': nrm(ks[12], (N_CONV, CONV_W, D_MODEL), CONV_W ** -0.5),
        'w_conv_out': nrm(ks[13], (N_CONV, D_MODEL, D_MODEL), D_MODEL ** -0.5),
        'w_up': nrm(ks[14], (DEPTH, D_MODEL, D_FF), D_MODEL ** -0.5),
        'w_down': nrm(ks[15], (DEPTH, D_FF, D_MODEL), D_FF ** -0.5),
    }


def reference(x_prompt, x_sample, cache_k, cache_v, state_conv, page_table, norm_mix, norm_ffn,
              norm_final, w_qkv, w_o, w_conv_in, conv_w, w_conv_out, w_up, w_down):
    page_sums_all = jnp.sum(cache_k, axis=2, dtype=jnp.float32)
    xp, xs = x_prompt, x_sample
    bp, sp = xp.shape[:2]
    bs, ss = xs.shape[:2]
    kp_l, vp_l, ks_l, vs_l, cp_l, cs_l = [], [], [], [], [], []
    for i in range(DEPTH):
        hp = rmsnorm(xp, norm_mix[i])
        hs = rmsnorm(xs, norm_mix[i])
        if i % 2 == 0:
            a = i // 2
            qp, kp, vp = qkv_split(hp, w_qkv[a])
            qs, kn, vn = qkv_split(hs, w_qkv[a])
            op = moba_prompt(qp, kp, vp).reshape(bp, sp, D_MODEL) @ w_o[a]
            os_ = moba_sample(qs, kn, vn, cache_k, cache_v, page_sums_all[a], a, page_table)
            os_ = os_.reshape(bs, ss, D_MODEL) @ w_o[a]
            kp_l.append(kp); vp_l.append(vp); ks_l.append(kn); vs_l.append(vn)
        else:
            c = i // 2
            zero_state = jnp.zeros((bp, CONV_W - 1, D_MODEL), xp.dtype)
            op, cp = short_conv_mixer(hp, zero_state, w_conv_in[c], conv_w[c], w_conv_out[c])
            os_, cs = short_conv_mixer(hs, state_conv[c], w_conv_in[c], conv_w[c], w_conv_out[c])
            cp_l.append(cp); cs_l.append(cs)
        xp = xp + op
        xs = xs + os_
        xp = xp + sq_relu_mlp(rmsnorm(xp, norm_ffn[i]), w_up[i], w_down[i])
        xs = xs + sq_relu_mlp(rmsnorm(xs, norm_ffn[i]), w_up[i], w_down[i])
    y_prompt = rmsnorm(xp, norm_final)
    y_sample = rmsnorm(xs, norm_final)
    return (y_prompt, y_sample, jnp.stack(kp_l), jnp.stack(vp_l), jnp.stack(ks_l), jnp.stack(vs_l),
            jnp.stack(cp_l), jnp.stack(cs_l))
```

```python
import functools

import jax
import jax.numpy as jnp
from jax import lax
from jax.experimental import pallas as pl
from jax.experimental.pallas import tpu as pltpu

F32 = jnp.float32
BF16 = jnp.bfloat16

N_HEADS = 8
BLOCK = 256
TOP_K = 3
PAGE_SIZE = 128
CONV_W = 3
EPS = 1e-6
LOG2E = 1.4426950408889634
NEG_BIG = -1e30

VMEM_LIMIT = 56 * 1024 * 1024


def _params(*sem):
    return pltpu.CompilerParams(dimension_semantics=sem, vmem_limit_bytes=VMEM_LIMIT)


def _rmsnorm(x, g):
    y = x * lax.rsqrt(jnp.mean(x * x, axis=-1, keepdims=True) + EPS)
    return y * g


def _top_k_rows(gate, n_valid):
    r = gate.shape[0]
    row = lax.broadcasted_iota(jnp.int32, gate.shape, 0).astype(F32)
    valid = row < n_valid
    g = jnp.where(valid, gate, -jnp.inf)
    sel = jnp.zeros(gate.shape, jnp.bool_)
    picks = []
    for _ in range(TOP_K):
        m = jnp.max(g, axis=0, keepdims=True)
        idx = jnp.min(jnp.where(g == m, row, float(r)), axis=0, keepdims=True)
        hit = row == idx
        sel = sel | hit
        g = jnp.where(hit, -jnp.inf, g)
        picks.append(idx)
    return sel & valid, jnp.concatenate(picks, axis=0)


def _qkv_prompt_kernel(x_ref, g_ref, w_ref, k_ref, v_ref, qt_ref, kb_ref, vt_ref, km_ref):
    d = x_ref.shape[1]
    tm = x_ref.shape[0]
    h = _rmsnorm(x_ref[...], g_ref[...]).astype(BF16)
    q = jnp.dot(h, w_ref[:, 0:d], preferred_element_type=F32)
    qt_ref[...] = q.T
    k = jnp.dot(h, w_ref[:, d:2 * d], preferred_element_type=F32)
    k_ref[...] = k
    kb_ref[...] = k.astype(BF16)
    km_ref[...] = jnp.sum(k.reshape(tm // BLOCK, BLOCK, d), axis=1) / BLOCK
    v = jnp.dot(h, w_ref[:, 2 * d:3 * d], preferred_element_type=F32)
    v_ref[...] = v
    vt_ref[...] = v.T.astype(BF16)


def _qkv_prompt(x, g, w_bf, tm=512):
    s, d = x.shape
    nt = s // tm
    row = pl.BlockSpec((tm, d), lambda i: (i, 0))
    col = pl.BlockSpec((d, tm), lambda i: (0, i))
    return pl.pallas_call(
        _qkv_prompt_kernel,
        grid=(nt,),
        in_specs=[row, pl.BlockSpec((1, d), lambda i: (0, 0)),
                  pl.BlockSpec((d, 3 * d), lambda i: (0, 0))],
        out_specs=[row, row, col, row, col,
                   pl.BlockSpec((None, tm // BLOCK, d), lambda i: (i, 0, 0))],
        out_shape=[jax.ShapeDtypeStruct((s, d), F32), jax.ShapeDtypeStruct((s, d), F32),
                   jax.ShapeDtypeStruct((d, s), F32), jax.ShapeDtypeStruct((s, d), BF16),
                   jax.ShapeDtypeStruct((d, s), BF16),
                   jax.ShapeDtypeStruct((nt, tm // BLOCK, d), F32)],
        compiler_params=_params("parallel"),
        name="qkv_prompt",
    )(x, g, w_bf)


def _proj_kernel(x_ref, g_ref, w_ref, o_ref):
    h = _rmsnorm(x_ref[...], g_ref[...]).astype(BF16)
    o_ref[...] = jnp.dot(h, w_ref[...], preferred_element_type=F32)


def _norm_proj(x, g, w_bf, tn=1024):
    m, d = x.shape
    n = w_bf.shape[1]
    return pl.pallas_call(
        _proj_kernel,
        grid=(n // tn,),
        in_specs=[pl.BlockSpec((m, d), lambda j: (0, 0)), pl.BlockSpec((1, d), lambda j: (0, 0)),
                  pl.BlockSpec((d, tn), lambda j: (0, j))],
        out_specs=pl.BlockSpec((m, tn), lambda j: (0, j)),
        out_shape=jax.ShapeDtypeStruct((m, n), F32),
        compiler_params=_params("parallel"),
        name="norm_proj_sample",
    )(x, g, w_bf)


def _attn_prompt_kernel(qt_ref, k_ref, vt_ref, km_ref, o_ref, bias_ref, *, scale):
    i = pl.program_id(1)
    nb = km_ref.shape[0]
    hd, tq = qt_ref.shape
    c = scale * LOG2E
    qt = qt_ref[...]
    gate = jnp.dot(km_ref[...], qt, precision=lax.Precision.HIGHEST,
                   preferred_element_type=F32)
    sel, _ = _top_k_rows(gate, i.astype(F32))
    bias_ref[...] = jnp.where(sel, 0.0, -jnp.inf)
    qtb = qt.astype(BF16)

    def step(s, ok, m, l, acc, vt):
        cm = jnp.where(ok, jnp.max(s, axis=0, keepdims=True), -jnp.inf)
        m_new = jnp.maximum(m, cm)
        p = jnp.exp2((s - jnp.where(ok, m_new, jnp.inf)) * c)
        alpha = jnp.exp2((m - m_new) * c)
        l = alpha * l + jnp.sum(p, axis=0, keepdims=True)
        acc = alpha * acc + jnp.dot(vt, p.astype(BF16), preferred_element_type=F32)
        return m_new, l, acc

    def body(j, carry):
        m, l, acc = carry
        off = pl.multiple_of(j * BLOCK, BLOCK)
        s = jnp.dot(k_ref[pl.ds(off, BLOCK), :], qtb, preferred_element_type=F32)
        ok = bias_ref[pl.ds(j, 1), :] == 0.0
        return step(s, ok, m, l, acc, vt_ref[:, pl.ds(off, BLOCK)])

    init = (jnp.full((1, tq), NEG_BIG, F32), jnp.zeros((1, tq), F32), jnp.zeros((hd, tq), F32))
    m, l, acc = lax.fori_loop(0, i, body, init)

    off = pl.multiple_of(i * BLOCK, BLOCK)
    s = jnp.dot(k_ref[pl.ds(off, BLOCK), :], qtb, preferred_element_type=F32)
    key = lax.broadcasted_iota(jnp.int32, s.shape, 0)
    qry = lax.broadcasted_iota(jnp.int32, s.shape, 1)
    s = jnp.where(key <= qry, s, -jnp.inf)
    ok = jnp.full((1, tq), True)
    m, l, acc = step(s, ok, m, l, acc, vt_ref[:, pl.ds(off, BLOCK)])
    o_ref[...] = (acc / l).T.astype(o_ref.dtype)


def _attn_prompt(qt, kb, vt, km):
    d, s = qt.shape
    hd = d // N_HEADS
    nb = s // BLOCK
    return pl.pallas_call(
        functools.partial(_attn_prompt_kernel, scale=hd ** -0.5),
        grid=(N_HEADS, nb),
        in_specs=[pl.BlockSpec((hd, BLOCK), lambda h, i: (h, i)),
                  pl.BlockSpec((s, hd), lambda h, i: (0, h)),
                  pl.BlockSpec((hd, s), lambda h, i: (h, 0)),
                  pl.BlockSpec((nb, hd), lambda h, i: (0, h))],
        out_specs=pl.BlockSpec((BLOCK, hd), lambda h, i: (i, h)),
        out_shape=jax.ShapeDtypeStruct((s, d), BF16),
        scratch_shapes=[pltpu.VMEM((nb, BLOCK), F32)],
        compiler_params=_params("parallel", "arbitrary"),
        name="attn_prompt",
    )(qt, kb, vt, km)


def _post_kernel(a_ref, x_ref, wo_ref, g_ref, wup_ref, wdn_ref, gf_ref, out_ref,
                 x1_ref, h_ref, acc_ref, *, final_norm):
    f = pl.program_id(1)

    @pl.when(f == 0)
    def _():
        x1 = x_ref[...] + jnp.dot(a_ref[...], wo_ref[...], preferred_element_type=F32)
        x1_ref[...] = x1
        h_ref[...] = _rmsnorm(x1, g_ref[...]).astype(BF16)
        acc_ref[...] = jnp.zeros_like(acc_ref)

    u = jnp.maximum(jnp.dot(h_ref[...], wup_ref[...], preferred_element_type=F32), 0.0)
    acc_ref[...] += jnp.dot((u * u).astype(BF16), wdn_ref[...], preferred_element_type=F32)

    @pl.when(f == pl.num_programs(1) - 1)
    def _():
        y = x1_ref[...] + acc_ref[...]
        if final_norm:
            y = _rmsnorm(y, gf_ref[...])
        out_ref[...] = y


def _post(a_bf, x, wo_bf, g_ffn, wup_bf, wdn_bf, g_final, final_norm, tm, tf=1024):
    s, d = x.shape
    ff = wup_bf.shape[1]
    tm = min(tm, s)
    row = lambda i, f: (i, 0)
    fix = lambda i, f: (0, 0)
    return pl.pallas_call(
        functools.partial(_post_kernel, final_norm=final_norm),
        grid=(s // tm, ff // tf),
        in_specs=[pl.BlockSpec((tm, d), row), pl.BlockSpec((tm, d), row),
                  pl.BlockSpec((d, d), fix), pl.BlockSpec((1, d), fix),
                  pl.BlockSpec((d, tf), lambda i, f: (0, f)),
                  pl.BlockSpec((tf, d), lambda i, f: (f, 0)),
                  pl.BlockSpec((1, d), fix)],
        out_specs=pl.BlockSpec((tm, d), row),
        out_shape=jax.ShapeDtypeStruct((s, d), F32),
        scratch_shapes=[pltpu.VMEM((tm, d), F32), pltpu.VMEM((tm, d), BF16),
                        pltpu.VMEM((tm, d), F32)],
        compiler_params=_params("parallel", "arbitrary"),
        name="post_mlp",
    )(a_bf, x, wo_bf, g_ffn, wup_bf, wdn_bf, g_final)


def _conv_prompt_kernel(x_ref, g_ref, w_ref, cw_ref, st_ref, a_ref, so_ref, carry_ref):
    tm, d = x_ref.shape

    @pl.when(pl.program_id(0) == 0)
    def _():
        carry_ref[...] = st_ref[...]

    h = _rmsnorm(x_ref[...], g_ref[...]).astype(BF16)
    gc = jnp.dot(h, w_ref[:, d:2 * d], preferred_element_type=F32)
    u = jnp.dot(h, w_ref[:, 2 * d:3 * d], preferred_element_type=F32)
    z = gc * u
    prev2 = carry_ref[0:1, :]
    prev1 = carry_ref[1:2, :]
    row = lax.broadcasted_iota(jnp.int32, (tm, d), 0)
    z1 = jnp.where(row == 0, prev1, pltpu.roll(z, 1, 0))
    z2 = jnp.where(row == 0, prev2, jnp.where(row == 1, prev1, pltpu.roll(z, 2, 0)))
    y = cw_ref[0:1, :] * z2
    y = y + cw_ref[1:2, :] * z1
    y = y + cw_ref[2:3, :] * z
    gb = jnp.dot(h, w_ref[:, 0:d], preferred_element_type=F32)
    a_ref[...] = (gb * y).astype(a_ref.dtype)
    tail = z[tm - (CONV_W - 1):tm, :]
    carry_ref[...] = tail
    so_ref[...] = tail


def _conv_prompt(x, g, w_bf, cw, state, tm=512):
    s, d = x.shape
    fix = lambda i: (0, 0)
    return pl.pallas_call(
        _conv_prompt_kernel,
        grid=(s // tm,),
        in_specs=[pl.BlockSpec((tm, d), lambda i: (i, 0)), pl.BlockSpec((1, d), fix),
                  pl.BlockSpec((d, 3 * d), fix), pl.BlockSpec((CONV_W, d), fix),
                  pl.BlockSpec((CONV_W - 1, d), fix)],
        out_specs=[pl.BlockSpec((tm, d), lambda i: (i, 0)), pl.BlockSpec((CONV_W - 1, d), fix)],
        out_shape=[jax.ShapeDtypeStruct((s, d), BF16), jax.ShapeDtypeStruct((CONV_W - 1, d), F32)],
        scratch_shapes=[pltpu.VMEM((CONV_W - 1, d), F32)],
        compiler_params=_params("arbitrary"),
        name="conv_prompt",
    )(x, g, w_bf, cw, state)


def _conv_sample_kernel(x_ref, g_ref, w_ref, cw_ref, s0_ref, s1_ref, a_ref, z_ref):
    d = x_ref.shape[1]
    h = _rmsnorm(x_ref[...], g_ref[...]).astype(BF16)
    gb = jnp.dot(h, w_ref[:, 0:d], preferred_element_type=F32)
    gc = jnp.dot(h, w_ref[:, d:2 * d], preferred_element_type=F32)
    u = jnp.dot(h, w_ref[:, 2 * d:3 * d], preferred_element_type=F32)
    z = gc * u
    y = cw_ref[0:1, :] * s0_ref[...]
    y = y + cw_ref[1:2, :] * s1_ref[...]
    y = y + cw_ref[2:3, :] * z
    a_ref[...] = (gb * y).astype(a_ref.dtype)
    z_ref[...] = z


def _conv_sample(x, g, w_bf, cw, s0, s1):
    m, d = x.shape
    return pl.pallas_call(
        _conv_sample_kernel,
        out_shape=[jax.ShapeDtypeStruct((m, d), BF16), jax.ShapeDtypeStruct((m, d), F32)],
        compiler_params=pltpu.CompilerParams(vmem_limit_bytes=VMEM_LIMIT),
        name="conv_sample",
    )(x, g, w_bf, cw, s0, s1)


PAGES_PER_STEP = 16


def _pagesum_kernel(pt_ref, *refs):
    pages, out_ref = refs[:PAGES_PER_STEP], refs[PAGES_PER_STEP]
    ppb = BLOCK // PAGE_SIZE
    for b in range(PAGES_PER_STEP // ppb):
        acc = jnp.sum(pages[b * ppb][...], axis=0)
        for r in range(1, ppb):
            acc = acc + jnp.sum(pages[b * ppb + r][...], axis=0)
        out_ref[b] = acc


def _pagesum(page_table_flat, cache, layer, n_seq, n_pages):
    nh, hd = cache.shape[-2:]
    ppb = BLOCK // PAGE_SIZE
    n_chunks = n_pages // PAGES_PER_STEP

    def page_spec(r):
        return pl.BlockSpec((None, None, PAGE_SIZE, nh, hd),
                            lambda b, c, pt: (layer, pt[b * n_pages + c * PAGES_PER_STEP + r], 0, 0, 0))

    return pl.pallas_call(
        _pagesum_kernel,
        grid_spec=pltpu.PrefetchScalarGridSpec(
            num_scalar_prefetch=1,
            grid=(n_seq, n_chunks),
            in_specs=[page_spec(r) for r in range(PAGES_PER_STEP)],
            out_specs=pl.BlockSpec((None, PAGES_PER_STEP // ppb, nh, hd),
                                   lambda b, c, pt: (b, c, 0, 0)),
        ),
        out_shape=jax.ShapeDtypeStruct((n_seq, n_pages // ppb, nh, hd), F32),
        compiler_params=_params("parallel", "arbitrary"),
        name="page_sums",
    )(page_table_flat, *([cache] * PAGES_PER_STEP))


def _select_sample_kernel(ks_ref, q_ref, sel_ref):
    nb = ks_ref.shape[0]
    prod = (ks_ref[...] / BLOCK) * q_ref[...]
    gate = jnp.sum(prod, axis=-1, keepdims=True)
    _, picks = _top_k_rows(gate, float(nb))
    sel_ref[...] = jnp.broadcast_to(picks, sel_ref.shape).astype(jnp.int32)


def _select_sample(ksum, q):
    n_seq, nb, nh, hd = ksum.shape
    return pl.pallas_call(
        _select_sample_kernel,
        grid=(n_seq,),
        in_specs=[pl.BlockSpec((None, nb, nh, hd), lambda b: (b, 0, 0, 0)),
                  pl.BlockSpec((None, nh, hd), lambda b: (b, 0, 0))],
        out_specs=pl.BlockSpec((None, TOP_K, nh, hd), lambda b: (b, 0, 0, 0)),
        out_shape=jax.ShapeDtypeStruct((n_seq, TOP_K, nh, hd), jnp.int32),
        compiler_params=_params("parallel"),
        name="select_sample",
    )(ksum, q.reshape(n_seq, nh, hd))


PAGES_PER_HEAD = TOP_K * (BLOCK // PAGE_SIZE)


def _attn_sample_kernel(pt_ref, sel_ref, q_ref, kn_ref, vn_ref, ck_ref, cv_ref, o_ref,
                        kbuf, vbuf, sem, *, scale, layer, n_pages):
    b = pl.program_id(0)
    n_seq = pl.num_programs(0)
    nh, hd = q_ref.shape
    ppb = BLOCK // PAGE_SIZE
    c = scale * LOG2E

    def copies(seq, slot):
        out = []
        for h in range(nh):
            for s in range(TOP_K):
                blk = sel_ref[(seq * TOP_K + s) * nh + h]
                for r in range(ppb):
                    page = pt_ref[seq * n_pages + blk * ppb + r]
                    i = h * PAGES_PER_HEAD + s * ppb + r
                    out.append(pltpu.make_async_copy(ck_ref.at[layer, page, :, h, :],
                                                     kbuf.at[slot, i], sem.at[slot, 0, i]))
                    out.append(pltpu.make_async_copy(cv_ref.at[layer, page, :, h, :],
                                                     vbuf.at[slot, i], sem.at[slot, 1, i]))
        return out

    slot = b % 2

    @pl.when(b == 0)
    def _():
        for cp in copies(b, slot):
            cp.start()

    @pl.when(b + 1 < n_seq)
    def _():
        for cp in copies(b + 1, 1 - slot):
            cp.start()

    for cp in copies(b, slot):
        cp.wait()

    for h in range(nh):
        qb = jnp.broadcast_to(q_ref[h:h + 1, :], (8, hd)).astype(BF16)
        pages = range(h * PAGES_PER_HEAD, (h + 1) * PAGES_PER_HEAD)
        logits = [lax.dot_general(qb, kbuf[slot, i].astype(BF16), (((1,), (1,)), ((), ())),
                                  preferred_element_type=F32) for i in pages]
        knb = kn_ref[h:h + 1, :].astype(BF16).astype(F32)
        own = jnp.sum(qb.astype(F32) * knb, axis=-1, keepdims=True)
        m = own
        for lg in logits:
            m = jnp.maximum(m, jnp.max(lg, axis=-1, keepdims=True))
        p_own = jnp.exp2((own - m) * c)
        l = p_own
        acc = p_own.astype(BF16).astype(F32) * vn_ref[h:h + 1, :].astype(BF16).astype(F32)
        for lg, i in zip(logits, pages):
            p = jnp.exp2((lg - m) * c)
            l = l + jnp.sum(p, axis=-1, keepdims=True)
            acc = acc + jnp.dot(p.astype(BF16), vbuf[slot, i].astype(BF16),
                                preferred_element_type=F32)
        o_ref[h:h + 1, :] = (acc / l)[0:1, :]


def _attn_sample(page_table_flat, sel_flat, q, kn, vn, cache_k, cache_v, layer, n_pages):
    n_seq, d = q.shape
    hd = d // N_HEADS
    vec = pl.BlockSpec((None, N_HEADS, hd), lambda b, pt, sel: (b, 0, 0))
    to3 = lambda a: a.reshape(n_seq, N_HEADS, hd)
    n_buf = N_HEADS * PAGES_PER_HEAD
    out = pl.pallas_call(
        functools.partial(_attn_sample_kernel, scale=hd ** -0.5, layer=layer, n_pages=n_pages),
        grid_spec=pltpu.PrefetchScalarGridSpec(
            num_scalar_prefetch=2,
            grid=(n_seq,),
            in_specs=[vec, vec, vec, pl.BlockSpec(memory_space=pl.ANY),
                      pl.BlockSpec(memory_space=pl.ANY)],
            out_specs=vec,
            scratch_shapes=[pltpu.VMEM((2, n_buf, PAGE_SIZE, hd), F32),
                            pltpu.VMEM((2, n_buf, PAGE_SIZE, hd), F32),
                            pltpu.SemaphoreType.DMA((2, 2, n_buf))],
        ),
        out_shape=jax.ShapeDtypeStruct((n_seq, N_HEADS, hd), F32),
        compiler_params=_params("arbitrary"),
        name="attn_sample",
    )(page_table_flat, sel_flat, to3(q), to3(kn), to3(vn), cache_k, cache_v)
    return out.reshape(n_seq, d).astype(BF16)


def kernel(x_prompt, x_sample, cache_k, cache_v, state_conv, page_table, norm_mix, norm_ffn,
           norm_final, w_qkv, w_o, w_conv_in, conv_w, w_conv_out, w_up, w_down):
    bp, sp, d = x_prompt.shape
    bs, ss, _ = x_sample.shape
    depth = norm_mix.shape[0]
    hd = d // N_HEADS
    n_pages = page_table.shape[1]
    assert bp == 1 and ss == 1 and sp % BLOCK == 0 and d == N_HEADS * hd
    assert n_pages * PAGE_SIZE % BLOCK == 0 and n_pages % PAGES_PER_STEP == 0
    assert cache_k.shape[2:] == (PAGE_SIZE, N_HEADS, hd) and CONV_W == 3

    xp = x_prompt.reshape(sp, d)
    xs = x_sample.reshape(bs, d)
    ck, cv = cache_k, cache_v
    pt_flat = page_table.reshape(-1)
    g_final = norm_final.reshape(1, d)
    bf = lambda w: w.astype(BF16)

    kp_l, vp_l, ks_l, vs_l, cp_l, cs_l = [], [], [], [], [], []
    for i in range(depth):
        g_mix = norm_mix[i].reshape(1, d)
        g_ffn = norm_ffn[i].reshape(1, d)
        last = i == depth - 1
        if i % 2 == 0:
            a = i // 2
            wqkv = bf(w_qkv[a])
            kp, vp, qt, kb, vt, km = _qkv_prompt(xp, g_mix, wqkv)
            ap = _attn_prompt(qt, kb, vt, km.reshape(sp // BLOCK, d))
            qkv_s = _norm_proj(xs, g_mix, wqkv)
            qs, kn, vn = qkv_s[:, :d], qkv_s[:, d:2 * d], qkv_s[:, 2 * d:]
            ksum = _pagesum(pt_flat, ck, a, bs, n_pages)
            sel = _select_sample(ksum, qs)[:, :, :, 0].reshape(-1)
            as_ = _attn_sample(pt_flat, sel, qs, kn, vn, ck, cv, a, n_pages)
            w_out = bf(w_o[a])
            kp_l.append(kp.reshape(bp, sp, N_HEADS, hd))
            vp_l.append(vp.reshape(bp, sp, N_HEADS, hd))
            ks_l.append(kn.reshape(bs, ss, N_HEADS, hd))
            vs_l.append(vn.reshape(bs, ss, N_HEADS, hd))
        else:
            c = i // 2
            win = bf(w_conv_in[c])
            zero_state = jnp.zeros((CONV_W - 1, d), F32)
            ap, cp = _conv_prompt(xp, g_mix, win, conv_w[c], zero_state)
            s0, s1 = state_conv[c, :, 0, :], state_conv[c, :, 1, :]
            as_, zs = _conv_sample(xs, g_mix, win, conv_w[c], s0, s1)
            w_out = bf(w_conv_out[c])
            cp_l.append(cp.reshape(bp, CONV_W - 1, d))
            cs_l.append(jnp.stack([s1, zs], axis=1))
        wup, wdn = bf(w_up[i]), bf(w_down[i])
        xp = _post(ap, xp, w_out, g_ffn, wup, wdn, g_final, last, tm=512)
        xs = _post(as_, xs, w_out, g_ffn, wup, wdn, g_final, last, tm=bs)
    return (xp.reshape(bp, sp, d), xs.reshape(bs, ss, d), jnp.stack(kp_l), jnp.stack(vp_l),
            jnp.stack(ks_l), jnp.stack(vs_l), jnp.stack(cp_l), jnp.stack(cs_l))
```

```python
import functools

import jax
import jax.numpy as jnp
from jax import lax
from jax.experimental import pallas as pl
from jax.experimental.pallas import tpu as pltpu

F32 = jnp.float32
BF16 = jnp.bfloat16

N_HEADS = 8
BLOCK = 256
TOP_K = 3
PAGE_SIZE = 128
CONV_W = 3
EPS = 1e-6
LOG2E = 1.4426950408889634
GROUP = 4
ONES_ROWS = 16

VMEM_LIMIT = 56 * 1024 * 1024


def _params(*sem):
    return pltpu.CompilerParams(dimension_semantics=sem, vmem_limit_bytes=VMEM_LIMIT)


def _rmsnorm(x, g):
    y = x * lax.rsqrt(jnp.mean(x * x, axis=-1, keepdims=True) + EPS)
    return y * g


def _top_k_rows(gate, n_valid):
    r = gate.shape[0]
    row = lax.broadcasted_iota(jnp.int32, gate.shape, 0).astype(F32)
    valid = row < n_valid
    g = jnp.where(valid, gate, -jnp.inf)
    sel = jnp.zeros(gate.shape, jnp.bool_)
    picks = []
    for _ in range(TOP_K):
        m = jnp.max(g, axis=0, keepdims=True)
        idx = jnp.min(jnp.where(g == m, row, float(r)), axis=0, keepdims=True)
        hit = row == idx
        sel = sel | hit
        g = jnp.where(hit, -jnp.inf, g)
        picks.append(idx)
    return sel & valid, jnp.concatenate(picks, axis=0)


def _qkv_prompt_kernel(x_ref, g_ref, w_ref, k_ref, v_ref, qt_ref, kb_ref, vt_ref, km_ref):
    d = x_ref.shape[1]
    tm = x_ref.shape[0]
    h = _rmsnorm(x_ref[...], g_ref[...]).astype(BF16)
    q = jnp.dot(h, w_ref[:, 0:d], preferred_element_type=F32)
    qt_ref[...] = q.T
    k = jnp.dot(h, w_ref[:, d:2 * d], preferred_element_type=F32)
    k_ref[...] = k
    kb_ref[...] = k.astype(BF16)
    km_ref[...] = jnp.sum(k.reshape(tm // BLOCK, BLOCK, d), axis=1) / BLOCK
    v = jnp.dot(h, w_ref[:, 2 * d:3 * d], preferred_element_type=F32)
    v_ref[...] = v
    vt = v.T.astype(BF16)
    hd = d // N_HEADS
    for hh in range(N_HEADS):
        vt_ref[hh, 0:hd, :] = vt[hh * hd:(hh + 1) * hd, :]
        vt_ref[hh, hd:, :] = jnp.ones((ONES_ROWS, tm), BF16)


def _qkv_prompt(x, g, w_bf, tm=512):
    s, d = x.shape
    nt = s // tm
    hd = d // N_HEADS
    row = pl.BlockSpec((tm, d), lambda i: (i, 0))
    col = pl.BlockSpec((d, tm), lambda i: (0, i))
    return pl.pallas_call(
        _qkv_prompt_kernel,
        grid=(nt,),
        in_specs=[row, pl.BlockSpec((1, d), lambda i: (0, 0)),
                  pl.BlockSpec((d, 3 * d), lambda i: (0, 0))],
        out_specs=[row, row, col, row,
                   pl.BlockSpec((N_HEADS, hd + ONES_ROWS, tm), lambda i: (0, 0, i)),
                   pl.BlockSpec((None, tm // BLOCK, d), lambda i: (i, 0, 0))],
        out_shape=[jax.ShapeDtypeStruct((s, d), F32), jax.ShapeDtypeStruct((s, d), F32),
                   jax.ShapeDtypeStruct((d, s), F32), jax.ShapeDtypeStruct((s, d), BF16),
                   jax.ShapeDtypeStruct((N_HEADS, hd + ONES_ROWS, s), BF16),
                   jax.ShapeDtypeStruct((nt, tm // BLOCK, d), F32)],
        compiler_params=_params("parallel"),
        name="qkv_prompt",
    )(x, g, w_bf)


def _proj_kernel(x_ref, g_ref, w_ref, o_ref):
    h = _rmsnorm(x_ref[...], g_ref[...]).astype(BF16)
    o_ref[...] = jnp.dot(h, w_ref[...], preferred_element_type=F32)


def _norm_proj(x, g, w_bf, tn=1024):
    m, d = x.shape
    n = w_bf.shape[1]
    return pl.pallas_call(
        _proj_kernel,
        grid=(n // tn,),
        in_specs=[pl.BlockSpec((m, d), lambda j: (0, 0)), pl.BlockSpec((1, d), lambda j: (0, 0)),
                  pl.BlockSpec((d, tn), lambda j: (0, j))],
        out_specs=pl.BlockSpec((m, tn), lambda j: (0, j)),
        out_shape=jax.ShapeDtypeStruct((m, n), F32),
        compiler_params=_params("parallel"),
        name="norm_proj_sample",
    )(x, g, w_bf)


def _sum_pages(page_refs, out_ref):
    ppb = BLOCK // PAGE_SIZE
    for b in range(len(page_refs) // ppb):
        acc = jnp.sum(page_refs[b * ppb][...], axis=0)
        for r in range(1, ppb):
            acc = acc + jnp.sum(page_refs[b * ppb + r][...], axis=0)
        out_ref[b] = acc


def _attn_prompt_kernel(pt_ref, qt_ref, k_ref, va_ref, km_ref, *refs, scale, n_page_refs):
    page_refs = refs[:n_page_refs]
    o_ref, ksum_ref, bias_ref, acc_ref, sa_ref, sb_ref, pa_ref, pb_ref = refs[n_page_refs:]
    _sum_pages(page_refs, ksum_ref)

    i = pl.program_id(1)
    nb = km_ref.shape[0]
    hd, tq = qt_ref.shape
    gk = GROUP * BLOCK
    n_groups = nb // GROUP
    c = scale * LOG2E
    qt = qt_ref[...]
    gate = jnp.dot(km_ref[...], qt, precision=lax.Precision.HIGHEST,
                   preferred_element_type=F32)
    sel, _ = _top_k_rows(gate, i.astype(F32))
    bias_ref[...] = jnp.where(sel, 0.0, -jnp.inf)
    qtb = qt.astype(BF16)

    def scores(g, s_ref):
        off = pl.multiple_of(g * gk, gk)
        s = jnp.dot(k_ref[pl.ds(off, gk), :], qtb, preferred_element_type=F32)
        s_ref[...] = s
        return jnp.max(s.reshape(GROUP, BLOCK, tq), axis=1)

    def weights(g, s_ref, cm, p_ref, m):
        bias = [bias_ref[pl.ds(g * GROUP + b, 1), :] for b in range(GROUP)]
        m_new = m
        for b in range(GROUP):
            m_new = jnp.maximum(m_new, cm[b:b + 1, :] + bias[b])
        for b in range(GROUP):
            shift = jnp.where(bias[b] == 0.0, m_new, jnp.inf)
            rows = slice(b * BLOCK, (b + 1) * BLOCK)
            p_ref[rows, :] = jnp.exp2((s_ref[rows, :] - shift) * c).astype(BF16)
        return m_new, jnp.exp2((m - m_new) * c)

    def values(g, p_ref, alpha):
        off = pl.multiple_of(g * gk, gk)
        acc_ref[...] = alpha * acc_ref[...] + jnp.dot(
            va_ref[:, pl.ds(off, gk)], p_ref[...], preferred_element_type=F32)

    off_own = pl.multiple_of(i * BLOCK, BLOCK)
    s_own = jnp.dot(k_ref[pl.ds(off_own, BLOCK), :], qtb, preferred_element_type=F32)
    key = lax.broadcasted_iota(jnp.int32, s_own.shape, 0)
    qry = lax.broadcasted_iota(jnp.int32, s_own.shape, 1)
    s_own = jnp.where(key <= qry, s_own, -jnp.inf)
    m0 = jnp.max(s_own, axis=0, keepdims=True)
    p_own = jnp.exp2((s_own - m0) * c).astype(BF16)
    acc_ref[...] = jnp.dot(va_ref[:, pl.ds(off_own, BLOCK)], p_own, preferred_element_type=F32)

    pb_ref[...] = jnp.zeros_like(pb_ref)
    n_pairs = (i + 2 * GROUP - 1) // (2 * GROUP)
    cm_a0 = scores(0, sa_ref)

    def pair(t, carry):
        m, alpha_b, cm_a = carry
        g0 = 2 * t
        cm_b = scores(g0 + 1, sb_ref)
        m, alpha_a = weights(g0, sa_ref, cm_a, pa_ref, m)
        values(jnp.maximum(g0 - 1, 0), pb_ref, alpha_b)
        cm_a = scores(jnp.minimum(g0 + 2, n_groups - 1), sa_ref)
        m, alpha_b = weights(g0 + 1, sb_ref, cm_b, pb_ref, m)
        values(g0, pa_ref, alpha_a)
        return m, alpha_b, cm_a

    m, alpha_b, _ = lax.fori_loop(0, n_pairs, pair, (m0, jnp.ones((1, tq), F32), cm_a0))
    values(jnp.maximum(2 * n_pairs - 1, 0), pb_ref, alpha_b)
    acc = acc_ref[...]
    o_ref[...] = (acc[0:hd, :] / acc[hd:hd + 1, :]).T.astype(o_ref.dtype)


def _attn_prompt(qt, kb, va, km, page_table_flat, cache, layer, n_seq, n_pages):
    d, s = qt.shape
    hd = d // N_HEADS
    nb = s // BLOCK
    assert nb % (2 * GROUP) == 0
    gk = GROUP * BLOCK
    steps = N_HEADS * nb
    ppb = BLOCK // PAGE_SIZE
    pps = n_seq * n_pages // steps
    assert pps * steps == n_seq * n_pages and pps % ppb == 0 and n_pages % pps == 0
    steps_per_seq = n_pages // pps

    def page_spec(r):
        return pl.BlockSpec((None, None, PAGE_SIZE, N_HEADS, hd),
                            lambda h, i, pt: (layer, pt[(h * nb + i) * pps + r], 0, 0, 0))

    ksum_spec = pl.BlockSpec(
        (None, pps // ppb, N_HEADS, hd),
        lambda h, i, pt: ((h * nb + i) // steps_per_seq, (h * nb + i) % steps_per_seq, 0, 0))
    return pl.pallas_call(
        functools.partial(_attn_prompt_kernel, scale=hd ** -0.5, n_page_refs=pps),
        grid_spec=pltpu.PrefetchScalarGridSpec(
            num_scalar_prefetch=1,
            grid=(N_HEADS, nb),
            in_specs=[pl.BlockSpec((hd, BLOCK), lambda h, i, pt: (h, i)),
                      pl.BlockSpec((s, hd), lambda h, i, pt: (0, h)),
                      pl.BlockSpec((None, hd + ONES_ROWS, s), lambda h, i, pt: (h, 0, 0)),
                      pl.BlockSpec((nb, hd), lambda h, i, pt: (0, h))]
                     + [page_spec(r) for r in range(pps)],
            out_specs=[pl.BlockSpec((BLOCK, hd), lambda h, i, pt: (i, h)), ksum_spec],
            scratch_shapes=[pltpu.VMEM((nb, BLOCK), F32),
                            pltpu.VMEM((hd + ONES_ROWS, BLOCK), F32),
                            pltpu.VMEM((gk, BLOCK), F32), pltpu.VMEM((gk, BLOCK), F32),
                            pltpu.VMEM((gk, BLOCK), BF16), pltpu.VMEM((gk, BLOCK), BF16)],
        ),
        out_shape=[jax.ShapeDtypeStruct((s, d), BF16),
                   jax.ShapeDtypeStruct((n_seq, n_pages // ppb, N_HEADS, hd), F32)],
        compiler_params=_params("arbitrary", "arbitrary"),
        name="attn_prompt",
    )(page_table_flat, qt, kb, va, km, *([cache] * pps))


def _post_kernel(a_ref, x_ref, wo_ref, g_ref, wup_ref, wdn_ref, gf_ref, out_ref,
                 x1_ref, h_ref, acc_ref, *, final_norm):
    f = pl.program_id(1)

    @pl.when(f == 0)
    def _():
        x1 = x_ref[...] + jnp.dot(a_ref[...], wo_ref[...], preferred_element_type=F32)
        x1_ref[...] = x1
        h_ref[...] = _rmsnorm(x1, g_ref[...]).astype(BF16)
        acc_ref[...] = jnp.zeros_like(acc_ref)

    u = jnp.maximum(jnp.dot(h_ref[...], wup_ref[...], preferred_element_type=F32), 0.0)
    acc_ref[...] += jnp.dot((u * u).astype(BF16), wdn_ref[...], preferred_element_type=F32)

    @pl.when(f == pl.num_programs(1) - 1)
    def _():
        y = x1_ref[...] + acc_ref[...]
        if final_norm:
            y = _rmsnorm(y, gf_ref[...])
        out_ref[...] = y


def _post(a_bf, x, wo_bf, g_ffn, wup_bf, wdn_bf, g_final, final_norm, tm, tf=1024):
    s, d = x.shape
    ff = wup_bf.shape[1]
    tm = min(tm, s)
    row = lambda i, f: (i, 0)
    fix = lambda i, f: (0, 0)
    return pl.pallas_call(
        functools.partial(_post_kernel, final_norm=final_norm),
        grid=(s // tm, ff // tf),
        in_specs=[pl.BlockSpec((tm, d), row), pl.BlockSpec((tm, d), row),
                  pl.BlockSpec((d, d), fix), pl.BlockSpec((1, d), fix),
                  pl.BlockSpec((d, tf), lambda i, f: (0, f)),
                  pl.BlockSpec((tf, d), lambda i, f: (f, 0)),
                  pl.BlockSpec((1, d), fix)],
        out_specs=pl.BlockSpec((tm, d), row),
        out_shape=jax.ShapeDtypeStruct((s, d), F32),
        scratch_shapes=[pltpu.VMEM((tm, d), F32), pltpu.VMEM((tm, d), BF16),
                        pltpu.VMEM((tm, d), F32)],
        compiler_params=_params("parallel", "arbitrary"),
        name="post_mlp",
    )(a_bf, x, wo_bf, g_ffn, wup_bf, wdn_bf, g_final)


def _conv_prompt_kernel(x_ref, g_ref, w_ref, cw_ref, st_ref, a_ref, so_ref, carry_ref):
    tm, d = x_ref.shape

    @pl.when(pl.program_id(0) == 0)
    def _():
        carry_ref[...] = st_ref[...]

    h = _rmsnorm(x_ref[...], g_ref[...]).astype(BF16)
    gc = jnp.dot(h, w_ref[:, d:2 * d], preferred_element_type=F32)
    u = jnp.dot(h, w_ref[:, 2 * d:3 * d], preferred_element_type=F32)
    z = gc * u
    prev2 = carry_ref[0:1, :]
    prev1 = carry_ref[1:2, :]
    row = lax.broadcasted_iota(jnp.int32, (tm, d), 0)
    z1 = jnp.where(row == 0, prev1, pltpu.roll(z, 1, 0))
    z2 = jnp.where(row == 0, prev2, jnp.where(row == 1, prev1, pltpu.roll(z, 2, 0)))
    y = cw_ref[0:1, :] * z2
    y = y + cw_ref[1:2, :] * z1
    y = y + cw_ref[2:3, :] * z
    gb = jnp.dot(h, w_ref[:, 0:d], preferred_element_type=F32)
    a_ref[...] = (gb * y).astype(a_ref.dtype)
    tail = z[tm - (CONV_W - 1):tm, :]
    carry_ref[...] = tail
    so_ref[...] = tail


def _conv_prompt(x, g, w_bf, cw, state, tm=512):
    s, d = x.shape
    fix = lambda i: (0, 0)
    return pl.pallas_call(
        _conv_prompt_kernel,
        grid=(s // tm,),
        in_specs=[pl.BlockSpec((tm, d), lambda i: (i, 0)), pl.BlockSpec((1, d), fix),
                  pl.BlockSpec((d, 3 * d), fix), pl.BlockSpec((CONV_W, d), fix),
                  pl.BlockSpec((CONV_W - 1, d), fix)],
        out_specs=[pl.BlockSpec((tm, d), lambda i: (i, 0)), pl.BlockSpec((CONV_W - 1, d), fix)],
        out_shape=[jax.ShapeDtypeStruct((s, d), BF16), jax.ShapeDtypeStruct((CONV_W - 1, d), F32)],
        scratch_shapes=[pltpu.VMEM((CONV_W - 1, d), F32)],
        compiler_params=_params("arbitrary"),
        name="conv_prompt",
    )(x, g, w_bf, cw, state)


def _conv_sample_kernel(x_ref, g_ref, w_ref, cw_ref, s0_ref, s1_ref, a_ref, z_ref):
    d = x_ref.shape[1]
    h = _rmsnorm(x_ref[...], g_ref[...]).astype(BF16)
    gb = jnp.dot(h, w_ref[:, 0:d], preferred_element_type=F32)
    gc = jnp.dot(h, w_ref[:, d:2 * d], preferred_element_type=F32)
    u = jnp.dot(h, w_ref[:, 2 * d:3 * d], preferred_element_type=F32)
    z = gc * u
    y = cw_ref[0:1, :] * s0_ref[...]
    y = y + cw_ref[1:2, :] * s1_ref[...]
    y = y + cw_ref[2:3, :] * z
    a_ref[...] = (gb * y).astype(a_ref.dtype)
    z_ref[...] = z


def _conv_sample(x, g, w_bf, cw, s0, s1):
    m, d = x.shape
    return pl.pallas_call(
        _conv_sample_kernel,
        out_shape=[jax.ShapeDtypeStruct((m, d), BF16), jax.ShapeDtypeStruct((m, d), F32)],
        compiler_params=pltpu.CompilerParams(vmem_limit_bytes=VMEM_LIMIT),
        name="conv_sample",
    )(x, g, w_bf, cw, s0, s1)


def _select_sample_kernel(ks_ref, q_ref, sel_ref):
    nb = ks_ref.shape[0]
    prod = (ks_ref[...] / BLOCK) * q_ref[...]
    gate = jnp.sum(prod, axis=-1, keepdims=True)
    _, picks = _top_k_rows(gate, float(nb))
    sel_ref[...] = jnp.broadcast_to(picks, sel_ref.shape).astype(jnp.int32)


def _select_sample(ksum, q):
    n_seq, nb, nh, hd = ksum.shape
    return pl.pallas_call(
        _select_sample_kernel,
        grid=(n_seq,),
        in_specs=[pl.BlockSpec((None, nb, nh, hd), lambda b: (b, 0, 0, 0)),
                  pl.BlockSpec((None, nh, hd), lambda b: (b, 0, 0))],
        out_specs=pl.BlockSpec((None, TOP_K, nh, hd), lambda b: (b, 0, 0, 0)),
        out_shape=jax.ShapeDtypeStruct((n_seq, TOP_K, nh, hd), jnp.int32),
        compiler_params=_params("parallel"),
        name="select_sample",
    )(ksum, q.reshape(n_seq, nh, hd))


PAGES_PER_HEAD = TOP_K * (BLOCK // PAGE_SIZE)


def _attn_sample_kernel(pt_ref, sel_ref, q_ref, kn_ref, vn_ref, ck_ref, cv_ref, o_ref,
                        kbuf, vbuf, sem, *, scale, layer, n_pages):
    b = pl.program_id(0)
    n_seq = pl.num_programs(0)
    nh, hd = q_ref.shape
    ppb = BLOCK // PAGE_SIZE
    c = scale * LOG2E

    def copies(seq, slot):
        out = []
        for h in range(nh):
            for s in range(TOP_K):
                blk = sel_ref[(seq * TOP_K + s) * nh + h]
                for r in range(ppb):
                    page = pt_ref[seq * n_pages + blk * ppb + r]
                    i = h * PAGES_PER_HEAD + s * ppb + r
                    out.append(pltpu.make_async_copy(ck_ref.at[layer, page, :, h, :],
                                                     kbuf.at[slot, i], sem.at[slot, 0, i]))
                    out.append(pltpu.make_async_copy(cv_ref.at[layer, page, :, h, :],
                                                     vbuf.at[slot, i], sem.at[slot, 1, i]))
        return out

    slot = b % 2

    @pl.when(b == 0)
    def _():
        for cp in copies(b, slot):
            cp.start()

    @pl.when(b + 1 < n_seq)
    def _():
        for cp in copies(b + 1, 1 - slot):
            cp.start()

    for cp in copies(b, slot):
        cp.wait()

    for h in range(nh):
        qb = jnp.broadcast_to(q_ref[h:h + 1, :], (8, hd)).astype(BF16)
        pages = range(h * PAGES_PER_HEAD, (h + 1) * PAGES_PER_HEAD)
        logits = [lax.dot_general(qb, kbuf[slot, i].astype(BF16), (((1,), (1,)), ((), ())),
                                  preferred_element_type=F32) for i in pages]
        knb = kn_ref[h:h + 1, :].astype(BF16).astype(F32)
        own = jnp.sum(qb.astype(F32) * knb, axis=-1, keepdims=True)
        m = own
        for lg in logits:
            m = jnp.maximum(m, jnp.max(lg, axis=-1, keepdims=True))
        p_own = jnp.exp2((own - m) * c)
        l = p_own
        acc = p_own.astype(BF16).astype(F32) * vn_ref[h:h + 1, :].astype(BF16).astype(F32)
        for lg, i in zip(logits, pages):
            p = jnp.exp2((lg - m) * c)
            l = l + jnp.sum(p, axis=-1, keepdims=True)
            acc = acc + jnp.dot(p.astype(BF16), vbuf[slot, i].astype(BF16),
                                preferred_element_type=F32)
        o_ref[h:h + 1, :] = (acc / l)[0:1, :]


def _attn_sample(page_table_flat, sel_flat, q, kn, vn, cache_k, cache_v, layer, n_pages):
    n_seq, d = q.shape
    hd = d // N_HEADS
    vec = pl.BlockSpec((None, N_HEADS, hd), lambda b, pt, sel: (b, 0, 0))
    to3 = lambda a: a.reshape(n_seq, N_HEADS, hd)
    n_buf = N_HEADS * PAGES_PER_HEAD
    out = pl.pallas_call(
        functools.partial(_attn_sample_kernel, scale=hd ** -0.5, layer=layer, n_pages=n_pages),
        grid_spec=pltpu.PrefetchScalarGridSpec(
            num_scalar_prefetch=2,
            grid=(n_seq,),
            in_specs=[vec, vec, vec, pl.BlockSpec(memory_space=pl.ANY),
                      pl.BlockSpec(memory_space=pl.ANY)],
            out_specs=vec,
            scratch_shapes=[pltpu.VMEM((2, n_buf, PAGE_SIZE, hd), F32),
                            pltpu.VMEM((2, n_buf, PAGE_SIZE, hd), F32),
                            pltpu.SemaphoreType.DMA((2, 2, n_buf))],
        ),
        out_shape=jax.ShapeDtypeStruct((n_seq, N_HEADS, hd), F32),
        compiler_params=_params("arbitrary"),
        name="attn_sample",
    )(page_table_flat, sel_flat, to3(q), to3(kn), to3(vn), cache_k, cache_v)
    return out.reshape(n_seq, d).astype(BF16)


def kernel(x_prompt, x_sample, cache_k, cache_v, state_conv, page_table, norm_mix, norm_ffn,
           norm_final, w_qkv, w_o, w_conv_in, conv_w, w_conv_out, w_up, w_down):
    bp, sp, d = x_prompt.shape
    bs, ss, _ = x_sample.shape
    depth = norm_mix.shape[0]
    hd = d // N_HEADS
    n_pages = page_table.shape[1]
    assert bp == 1 and ss == 1 and sp % BLOCK == 0 and d == N_HEADS * hd
    assert n_pages * PAGE_SIZE % BLOCK == 0
    assert cache_k.shape[2:] == (PAGE_SIZE, N_HEADS, hd) and CONV_W == 3

    xp = x_prompt.reshape(sp, d)
    xs = x_sample.reshape(bs, d)
    ck, cv = cache_k, cache_v
    pt_flat = page_table.reshape(-1)
    g_final = norm_final.reshape(1, d)
    bf = lambda w: w.astype(BF16)

    kp_l, vp_l, ks_l, vs_l, cp_l, cs_l = [], [], [], [], [], []
    for i in range(depth):
        g_mix = norm_mix[i].reshape(1, d)
        g_ffn = norm_ffn[i].reshape(1, d)
        last = i == depth - 1
        if i % 2 == 0:
            a = i // 2
            wqkv = bf(w_qkv[a])
            kp, vp, qt, kb, va, km = _qkv_prompt(xp, g_mix, wqkv)
            ap, ksum = _attn_prompt(qt, kb, va, km.reshape(sp // BLOCK, d), pt_flat, ck, a,
                                    bs, n_pages)
            qkv_s = _norm_proj(xs, g_mix, wqkv)
            qs, kn, vn = qkv_s[:, :d], qkv_s[:, d:2 * d], qkv_s[:, 2 * d:]
            sel = _select_sample(ksum, qs)[:, :, :, 0].reshape(-1)
            as_ = _attn_sample(pt_flat, sel, qs, kn, vn, ck, cv, a, n_pages)
            w_out = bf(w_o[a])
            kp_l.append(kp.reshape(bp, sp, N_HEADS, hd))
            vp_l.append(vp.reshape(bp, sp, N_HEADS, hd))
            ks_l.append(kn.reshape(bs, ss, N_HEADS, hd))
            vs_l.append(vn.reshape(bs, ss, N_HEADS, hd))
        else:
            c = i // 2
            win = bf(w_conv_in[c])
            zero_state = jnp.zeros((CONV_W - 1, d), F32)
            ap, cp = _conv_prompt(xp, g_mix, win, conv_w[c], zero_state)
            s0, s1 = state_conv[c, :, 0, :], state_conv[c, :, 1, :]
            as_, zs = _conv_sample(xs, g_mix, win, conv_w[c], s0, s1)
            w_out = bf(w_conv_out[c])
            cp_l.append(cp.reshape(bp, CONV_W - 1, d))
            cs_l.append(jnp.stack([s1, zs], axis=1))
        wup, wdn = bf(w_up[i]), bf(w_down[i])
        xp = _post(ap, xp, w_out, g_ffn, wup, wdn, g_final, last, tm=512)
        xs = _post(as_, xs, w_out, g_ffn, wup, wdn, g_final, last, tm=bs)
    return (xp.reshape(bp, sp, d), xs.reshape(bs, ss, d), jnp.stack(kp_l), jnp.stack(vp_l),
            jnp.stack(ks_l), jnp.stack(vs_l), jnp.stack(cp_l), jnp.stack(cs_l))
```

```python
import functools

import jax
import jax.numpy as jnp
from jax import lax
from jax.experimental import pallas as pl
from jax.experimental.pallas import tpu as pltpu

F32 = jnp.float32
BF16 = jnp.bfloat16

N_HEADS = 8
BLOCK = 256
TOP_K = 3
PAGE_SIZE = 128
CONV_W = 3
EPS = 1e-6
LOG2E = 1.4426950408889634
GROUP = 4
Q_BLOCKS = 2
ONES_ROWS = 16

VMEM_LIMIT = 56 * 1024 * 1024


def _params(*sem):
    return pltpu.CompilerParams(dimension_semantics=sem, vmem_limit_bytes=VMEM_LIMIT)


def _rmsnorm(x, g):
    y = x * lax.rsqrt(jnp.mean(x * x, axis=-1, keepdims=True) + EPS)
    return y * g


def _top_k_rows(gate, n_valid):
    r = gate.shape[0]
    row = lax.broadcasted_iota(jnp.int32, gate.shape, 0).astype(F32)
    valid = row < n_valid
    g = jnp.where(valid, gate, -jnp.inf)
    sel = jnp.zeros(gate.shape, jnp.bool_)
    picks = []
    for _ in range(TOP_K):
        m = jnp.max(g, axis=0, keepdims=True)
        idx = jnp.min(jnp.where(g == m, row, float(r)), axis=0, keepdims=True)
        hit = row == idx
        sel = sel | hit
        g = jnp.where(hit, -jnp.inf, g)
        picks.append(idx)
    return sel & valid, jnp.concatenate(picks, axis=0)


def _qkv_prompt_kernel(x_ref, g_ref, w_ref, k_ref, v_ref, qt_ref, kb_ref, vt_ref, km_ref):
    d = x_ref.shape[1]
    tm = x_ref.shape[0]
    h = _rmsnorm(x_ref[...], g_ref[...]).astype(BF16)
    q = jnp.dot(h, w_ref[:, 0:d], preferred_element_type=F32)
    qt_ref[...] = q.T
    k = jnp.dot(h, w_ref[:, d:2 * d], preferred_element_type=F32)
    k_ref[...] = k
    kb_ref[...] = k.astype(BF16)
    km_ref[...] = jnp.sum(k.reshape(tm // BLOCK, BLOCK, d), axis=1) / BLOCK
    v = jnp.dot(h, w_ref[:, 2 * d:3 * d], preferred_element_type=F32)
    v_ref[...] = v
    vt = v.T.astype(BF16)
    hd = d // N_HEADS
    for hh in range(N_HEADS):
        vt_ref[hh, 0:hd, :] = vt[hh * hd:(hh + 1) * hd, :]
        vt_ref[hh, hd:, :] = jnp.ones((ONES_ROWS, tm), BF16)


def _qkv_prompt(x, g, w_bf, tm=512):
    s, d = x.shape
    nt = s // tm
    hd = d // N_HEADS
    row = pl.BlockSpec((tm, d), lambda i: (i, 0))
    col = pl.BlockSpec((d, tm), lambda i: (0, i))
    return pl.pallas_call(
        _qkv_prompt_kernel,
        grid=(nt,),
        in_specs=[row, pl.BlockSpec((1, d), lambda i: (0, 0)),
                  pl.BlockSpec((d, 3 * d), lambda i: (0, 0))],
        out_specs=[row, row, col, row,
                   pl.BlockSpec((N_HEADS, hd + ONES_ROWS, tm), lambda i: (0, 0, i)),
                   pl.BlockSpec((None, tm // BLOCK, d), lambda i: (i, 0, 0))],
        out_shape=[jax.ShapeDtypeStruct((s, d), F32), jax.ShapeDtypeStruct((s, d), F32),
                   jax.ShapeDtypeStruct((d, s), F32), jax.ShapeDtypeStruct((s, d), BF16),
                   jax.ShapeDtypeStruct((N_HEADS, hd + ONES_ROWS, s), BF16),
                   jax.ShapeDtypeStruct((nt, tm // BLOCK, d), F32)],
        compiler_params=_params("parallel"),
        name="qkv_prompt",
    )(x, g, w_bf)


def _proj_kernel(x_ref, g_ref, w_ref, o_ref):
    h = _rmsnorm(x_ref[...], g_ref[...]).astype(BF16)
    o_ref[...] = jnp.dot(h, w_ref[...], preferred_element_type=F32)


def _norm_proj(x, g, w_bf, tn=1024):
    m, d = x.shape
    n = w_bf.shape[1]
    return pl.pallas_call(
        _proj_kernel,
        grid=(n // tn,),
        in_specs=[pl.BlockSpec((m, d), lambda j: (0, 0)), pl.BlockSpec((1, d), lambda j: (0, 0)),
                  pl.BlockSpec((d, tn), lambda j: (0, j))],
        out_specs=pl.BlockSpec((m, tn), lambda j: (0, j)),
        out_shape=jax.ShapeDtypeStruct((m, n), F32),
        compiler_params=_params("parallel"),
        name="norm_proj_sample",
    )(x, g, w_bf)


def _sum_pages(page_refs, out_ref):
    ppb = BLOCK // PAGE_SIZE
    for b in range(len(page_refs) // ppb):
        acc = jnp.sum(page_refs[b * ppb][...], axis=0)
        for r in range(1, ppb):
            acc = acc + jnp.sum(page_refs[b * ppb + r][...], axis=0)
        out_ref[b] = acc


def _attn_prompt_kernel(pt_ref, qt_ref, k_ref, va_ref, km_ref, *refs, scale, n_page_refs):
    page_refs = refs[:n_page_refs]
    o_ref, ksum_ref, bias_ref, acc_ref, sa_ref, sb_ref, pa_ref, pb_ref = refs[n_page_refs:]
    _sum_pages(page_refs, ksum_ref)

    nb = km_ref.shape[0]
    hd, tq = qt_ref.shape
    b0 = pl.program_id(1) * Q_BLOCKS
    gk = GROUP * BLOCK
    n_groups = nb // GROUP
    c = scale * LOG2E
    qt = qt_ref[...]
    gate = jnp.dot(km_ref[...], qt, precision=lax.Precision.HIGHEST,
                   preferred_element_type=F32)
    qcol = lax.broadcasted_iota(jnp.int32, (1, tq), 1)
    own = b0 + qcol // BLOCK
    sel, _ = _top_k_rows(gate, own.astype(F32))
    picked = jnp.where(sel, 0.0, -jnp.inf)
    blk = lax.broadcasted_iota(jnp.int32, (nb, tq), 0)
    bias_ref[...] = jnp.where(blk < b0, picked, -jnp.inf)
    qtb = qt.astype(BF16)

    def scores(g, s_ref):
        off = pl.multiple_of(g * gk, gk)
        s_ref[...] = jnp.dot(k_ref[pl.ds(off, gk), :], qtb, preferred_element_type=F32)
        return jnp.concatenate(
            [jnp.max(s_ref[b * BLOCK:(b + 1) * BLOCK, :], axis=0, keepdims=True)
             for b in range(GROUP)], axis=0)

    def weights(g, s_ref, cm, p_ref, m):
        bias = [bias_ref[pl.ds(g * GROUP + b, 1), :] for b in range(GROUP)]
        m_new = m
        for b in range(GROUP):
            m_new = jnp.maximum(m_new, cm[b:b + 1, :] + bias[b])
        for b in range(GROUP):
            shift = jnp.where(bias[b] == 0.0, m_new, jnp.inf)
            rows = slice(b * BLOCK, (b + 1) * BLOCK)
            p_ref[rows, :] = jnp.exp2((s_ref[rows, :] - shift) * c).astype(BF16)
        return m_new, jnp.exp2((m - m_new) * c)

    def values(g, p_ref, alpha):
        off = pl.multiple_of(g * gk, gk)
        acc_ref[...] = alpha * acc_ref[...] + jnp.dot(
            va_ref[:, pl.ds(off, gk)], p_ref[...], preferred_element_type=F32)

    off_own = pl.multiple_of(b0 * BLOCK, tq)
    s_own = jnp.dot(k_ref[pl.ds(off_own, tq), :], qtb, preferred_element_type=F32)
    key = lax.broadcasted_iota(jnp.int32, s_own.shape, 0)
    qry = lax.broadcasted_iota(jnp.int32, s_own.shape, 1)
    picked_first = jnp.max(jnp.where(blk == b0, picked, -jnp.inf), axis=0, keepdims=True)
    first_ok = jnp.where(qcol < BLOCK, 0.0, picked_first)
    s_own = jnp.where(key <= qry, s_own, -jnp.inf) + jnp.where(key < BLOCK, first_ok, 0.0)
    m0 = jnp.max(s_own, axis=0, keepdims=True)
    p_own = jnp.exp2((s_own - m0) * c).astype(BF16)
    acc_ref[...] = jnp.dot(va_ref[:, pl.ds(off_own, tq)], p_own, preferred_element_type=F32)

    pb_ref[...] = jnp.zeros_like(pb_ref)
    n_pairs = (b0 + 2 * GROUP - 1) // (2 * GROUP)
    cm_a0 = scores(0, sa_ref)

    def pair(t, carry):
        m, alpha_b, cm_a = carry
        g0 = 2 * t
        cm_b = scores(g0 + 1, sb_ref)
        m, alpha_a = weights(g0, sa_ref, cm_a, pa_ref, m)
        values(jnp.maximum(g0 - 1, 0), pb_ref, alpha_b)
        cm_a = scores(jnp.minimum(g0 + 2, n_groups - 1), sa_ref)
        m, alpha_b = weights(g0 + 1, sb_ref, cm_b, pb_ref, m)
        values(g0, pa_ref, alpha_a)
        return m, alpha_b, cm_a

    m, alpha_b, _ = lax.fori_loop(0, n_pairs, pair, (m0, jnp.ones((1, tq), F32), cm_a0))
    values(jnp.maximum(2 * n_pairs - 1, 0), pb_ref, alpha_b)
    acc = acc_ref[...]
    o_ref[...] = (acc[0:hd, :] / acc[hd:hd + 1, :]).T.astype(o_ref.dtype)


def _attn_prompt(qt, kb, va, km, page_table_flat, cache, layer, n_seq, n_pages):
    d, s = qt.shape
    hd = d // N_HEADS
    nb = s // BLOCK
    assert nb % (2 * GROUP) == 0 and Q_BLOCKS == 2
    gk = GROUP * BLOCK
    tq = Q_BLOCKS * BLOCK
    nt = nb // Q_BLOCKS
    steps = N_HEADS * nt
    ppb = BLOCK // PAGE_SIZE
    pps = n_seq * n_pages // steps
    assert pps * steps == n_seq * n_pages and pps % ppb == 0 and n_pages % pps == 0
    steps_per_seq = n_pages // pps

    def page_spec(r):
        return pl.BlockSpec((None, None, PAGE_SIZE, N_HEADS, hd),
                            lambda h, i, pt: (layer, pt[(h * nt + i) * pps + r], 0, 0, 0))

    ksum_spec = pl.BlockSpec(
        (None, pps // ppb, N_HEADS, hd),
        lambda h, i, pt: ((h * nt + i) // steps_per_seq, (h * nt + i) % steps_per_seq, 0, 0))
    return pl.pallas_call(
        functools.partial(_attn_prompt_kernel, scale=hd ** -0.5, n_page_refs=pps),
        grid_spec=pltpu.PrefetchScalarGridSpec(
            num_scalar_prefetch=1,
            grid=(N_HEADS, nt),
            in_specs=[pl.BlockSpec((hd, tq), lambda h, i, pt: (h, i)),
                      pl.BlockSpec((s, hd), lambda h, i, pt: (0, h)),
                      pl.BlockSpec((None, hd + ONES_ROWS, s), lambda h, i, pt: (h, 0, 0)),
                      pl.BlockSpec((nb, hd), lambda h, i, pt: (0, h))]
                     + [page_spec(r) for r in range(pps)],
            out_specs=[pl.BlockSpec((tq, hd), lambda h, i, pt: (i, h)), ksum_spec],
            scratch_shapes=[pltpu.VMEM((nb, tq), F32),
                            pltpu.VMEM((hd + ONES_ROWS, tq), F32),
                            pltpu.VMEM((gk, tq), F32), pltpu.VMEM((gk, tq), F32),
                            pltpu.VMEM((gk, tq), BF16), pltpu.VMEM((gk, tq), BF16)],
        ),
        out_shape=[jax.ShapeDtypeStruct((s, d), BF16),
                   jax.ShapeDtypeStruct((n_seq, n_pages // ppb, N_HEADS, hd), F32)],
        compiler_params=_params("arbitrary", "arbitrary"),
        name="attn_prompt",
    )(page_table_flat, qt, kb, va, km, *([cache] * pps))


def _post_kernel(a_ref, x_ref, wo_ref, g_ref, wup_ref, wdn_ref, gf_ref, out_ref,
                 *, final_norm, tf):
    ff = wup_ref.shape[1]
    y = x_ref[...] + jnp.dot(a_ref[...], wo_ref[...], preferred_element_type=F32)
    h = _rmsnorm(y, g_ref[...]).astype(BF16)
    for f in range(ff // tf):
        u = jnp.dot(h, wup_ref[:, f * tf:(f + 1) * tf], preferred_element_type=F32)
        u = jnp.maximum(u, 0.0)
        y = y + jnp.dot((u * u).astype(BF16), wdn_ref[f * tf:(f + 1) * tf, :],
                        preferred_element_type=F32)
    if final_norm:
        y = _rmsnorm(y, gf_ref[...])
    out_ref[...] = y


def _post(a_bf, x, wo_bf, g_ffn, wup_bf, wdn_bf, g_final, final_norm, tm, tf=1024):
    s, d = x.shape
    ff = wup_bf.shape[1]
    tm = min(tm, s)
    row = lambda i: (i, 0)
    fix = lambda i: (0, 0)
    once = pl.Buffered(1)
    return pl.pallas_call(
        functools.partial(_post_kernel, final_norm=final_norm, tf=tf),
        grid=(s // tm,),
        in_specs=[pl.BlockSpec((tm, d), row), pl.BlockSpec((tm, d), row),
                  pl.BlockSpec((d, d), fix, pipeline_mode=once), pl.BlockSpec((1, d), fix),
                  pl.BlockSpec((d, ff), fix, pipeline_mode=once),
                  pl.BlockSpec((ff, d), fix, pipeline_mode=once),
                  pl.BlockSpec((1, d), fix)],
        out_specs=pl.BlockSpec((tm, d), row),
        out_shape=jax.ShapeDtypeStruct((s, d), F32),
        compiler_params=_params("parallel"),
        name="post_mlp",
    )(a_bf, x, wo_bf, g_ffn, wup_bf, wdn_bf, g_final)


def _conv_prompt_kernel(x_ref, g_ref, w_ref, cw_ref, st_ref, a_ref, so_ref, carry_ref):
    tm, d = x_ref.shape

    @pl.when(pl.program_id(0) == 0)
    def _():
        carry_ref[...] = st_ref[...]

    h = _rmsnorm(x_ref[...], g_ref[...]).astype(BF16)
    gc = jnp.dot(h, w_ref[:, d:2 * d], preferred_element_type=F32)
    u = jnp.dot(h, w_ref[:, 2 * d:3 * d], preferred_element_type=F32)
    z = gc * u
    prev2 = carry_ref[0:1, :]
    prev1 = carry_ref[1:2, :]
    row = lax.broadcasted_iota(jnp.int32, (tm, d), 0)
    z1 = jnp.where(row == 0, prev1, pltpu.roll(z, 1, 0))
    z2 = jnp.where(row == 0, prev2, jnp.where(row == 1, prev1, pltpu.roll(z, 2, 0)))
    y = cw_ref[0:1, :] * z2
    y = y + cw_ref[1:2, :] * z1
    y = y + cw_ref[2:3, :] * z
    gb = jnp.dot(h, w_ref[:, 0:d], preferred_element_type=F32)
    a_ref[...] = (gb * y).astype(a_ref.dtype)
    tail = z[tm - (CONV_W - 1):tm, :]
    carry_ref[...] = tail
    so_ref[...] = tail


def _conv_prompt(x, g, w_bf, cw, state, tm=512):
    s, d = x.shape
    fix = lambda i: (0, 0)
    return pl.pallas_call(
        _conv_prompt_kernel,
        grid=(s // tm,),
        in_specs=[pl.BlockSpec((tm, d), lambda i: (i, 0)), pl.BlockSpec((1, d), fix),
                  pl.BlockSpec((d, 3 * d), fix), pl.BlockSpec((CONV_W, d), fix),
                  pl.BlockSpec((CONV_W - 1, d), fix)],
        out_specs=[pl.BlockSpec((tm, d), lambda i: (i, 0)), pl.BlockSpec((CONV_W - 1, d), fix)],
        out_shape=[jax.ShapeDtypeStruct((s, d), BF16), jax.ShapeDtypeStruct((CONV_W - 1, d), F32)],
        scratch_shapes=[pltpu.VMEM((CONV_W - 1, d), F32)],
        compiler_params=_params("arbitrary"),
        name="conv_prompt",
    )(x, g, w_bf, cw, state)


def _conv_sample_kernel(x_ref, g_ref, w_ref, cw_ref, s0_ref, s1_ref, a_ref, z_ref):
    d = x_ref.shape[1]
    h = _rmsnorm(x_ref[...], g_ref[...]).astype(BF16)
    gb = jnp.dot(h, w_ref[:, 0:d], preferred_element_type=F32)
    gc = jnp.dot(h, w_ref[:, d:2 * d], preferred_element_type=F32)
    u = jnp.dot(h, w_ref[:, 2 * d:3 * d], preferred_element_type=F32)
    z = gc * u
    y = cw_ref[0:1, :] * s0_ref[...]
    y = y + cw_ref[1:2, :] * s1_ref[...]
    y = y + cw_ref[2:3, :] * z
    a_ref[...] = (gb * y).astype(a_ref.dtype)
    z_ref[...] = z


def _conv_sample(x, g, w_bf, cw, s0, s1):
    m, d = x.shape
    return pl.pallas_call(
        _conv_sample_kernel,
        out_shape=[jax.ShapeDtypeStruct((m, d), BF16), jax.ShapeDtypeStruct((m, d), F32)],
        compiler_params=pltpu.CompilerParams(vmem_limit_bytes=VMEM_LIMIT),
        name="conv_sample",
    )(x, g, w_bf, cw, s0, s1)


def _select_sample_kernel(ks_ref, q_ref, sel_ref):
    nb = ks_ref.shape[0]
    prod = (ks_ref[...] / BLOCK) * q_ref[...]
    gate = jnp.sum(prod, axis=-1, keepdims=True)
    _, picks = _top_k_rows(gate, float(nb))
    sel_ref[...] = jnp.broadcast_to(picks, sel_ref.shape).astype(jnp.int32)


def _select_sample(ksum, q):
    n_seq, nb, nh, hd = ksum.shape
    return pl.pallas_call(
        _select_sample_kernel,
        grid=(n_seq,),
        in_specs=[pl.BlockSpec((None, nb, nh, hd), lambda b: (b, 0, 0, 0)),
                  pl.BlockSpec((None, nh, hd), lambda b: (b, 0, 0))],
        out_specs=pl.BlockSpec((None, TOP_K, nh, hd), lambda b: (b, 0, 0, 0)),
        out_shape=jax.ShapeDtypeStruct((n_seq, TOP_K, nh, hd), jnp.int32),
        compiler_params=_params("parallel"),
        name="select_sample",
    )(ksum, q.reshape(n_seq, nh, hd))


PAGES_PER_HEAD = TOP_K * (BLOCK // PAGE_SIZE)


def _attn_sample_kernel(pt_ref, sel_ref, q_ref, kn_ref, vn_ref, ck_ref, cv_ref, o_ref,
                        kbuf, vbuf, sem, *, scale, layer, n_pages):
    b = pl.program_id(0)
    n_seq = pl.num_programs(0)
    nh, hd = q_ref.shape
    ppb = BLOCK // PAGE_SIZE
    c = scale * LOG2E

    def copies(seq, slot):
        out = []
        for h in range(nh):
            for s in range(TOP_K):
                blk = sel_ref[(seq * TOP_K + s) * nh + h]
                for r in range(ppb):
                    page = pt_ref[seq * n_pages + blk * ppb + r]
                    i = h * PAGES_PER_HEAD + s * ppb + r
                    out.append(pltpu.make_async_copy(ck_ref.at[layer, page, :, h, :],
                                                     kbuf.at[slot, i], sem.at[slot, 0, i]))
                    out.append(pltpu.make_async_copy(cv_ref.at[layer, page, :, h, :],
                                                     vbuf.at[slot, i], sem.at[slot, 1, i]))
        return out

    slot = b % 2

    @pl.when(b == 0)
    def _():
        for cp in copies(b, slot):
            cp.start()

    @pl.when(b + 1 < n_seq)
    def _():
        for cp in copies(b + 1, 1 - slot):
            cp.start()

    for cp in copies(b, slot):
        cp.wait()

    for h in range(nh):
        qb = jnp.broadcast_to(q_ref[h:h + 1, :], (8, hd)).astype(BF16)
        pages = range(h * PAGES_PER_HEAD, (h + 1) * PAGES_PER_HEAD)
        logits = [lax.dot_general(qb, kbuf[slot, i].astype(BF16), (((1,), (1,)), ((), ())),
                                  preferred_element_type=F32) for i in pages]
        knb = kn_ref[h:h + 1, :].astype(BF16).astype(F32)
        own = jnp.sum(qb.astype(F32) * knb, axis=-1, keepdims=True)
        m = own
        for lg in logits:
            m = jnp.maximum(m, jnp.max(lg, axis=-1, keepdims=True))
        p_own = jnp.exp2((own - m) * c)
        l = p_own
        acc = p_own.astype(BF16).astype(F32) * vn_ref[h:h + 1, :].astype(BF16).astype(F32)
        for lg, i in zip(logits, pages):
            p = jnp.exp2((lg - m) * c)
            l = l + jnp.sum(p, axis=-1, keepdims=True)
            acc = acc + jnp.dot(p.astype(BF16), vbuf[slot, i].astype(BF16),
                                preferred_element_type=F32)
        o_ref[h:h + 1, :] = (acc / l)[0:1, :]


def _attn_sample(page_table_flat, sel_flat, q, kn, vn, cache_k, cache_v, layer, n_pages):
    n_seq, d = q.shape
    hd = d // N_HEADS
    vec = pl.BlockSpec((None, N_HEADS, hd), lambda b, pt, sel: (b, 0, 0))
    to3 = lambda a: a.reshape(n_seq, N_HEADS, hd)
    n_buf = N_HEADS * PAGES_PER_HEAD
    out = pl.pallas_call(
        functools.partial(_attn_sample_kernel, scale=hd ** -0.5, layer=layer, n_pages=n_pages),
        grid_spec=pltpu.PrefetchScalarGridSpec(
            num_scalar_prefetch=2,
            grid=(n_seq,),
            in_specs=[vec, vec, vec, pl.BlockSpec(memory_space=pl.ANY),
                      pl.BlockSpec(memory_space=pl.ANY)],
            out_specs=vec,
            scratch_shapes=[pltpu.VMEM((2, n_buf, PAGE_SIZE, hd), F32),
                            pltpu.VMEM((2, n_buf, PAGE_SIZE, hd), F32),
                            pltpu.SemaphoreType.DMA((2, 2, n_buf))],
        ),
        out_shape=jax.ShapeDtypeStruct((n_seq, N_HEADS, hd), F32),
        compiler_params=_params("arbitrary"),
        name="attn_sample",
    )(page_table_flat, sel_flat, to3(q), to3(kn), to3(vn), cache_k, cache_v)
    return out.reshape(n_seq, d).astype(BF16)


def kernel(x_prompt, x_sample, cache_k, cache_v, state_conv, page_table, norm_mix, norm_ffn,
           norm_final, w_qkv, w_o, w_conv_in, conv_w, w_conv_out, w_up, w_down):
    bp, sp, d = x_prompt.shape
    bs, ss, _ = x_sample.shape
    depth = norm_mix.shape[0]
    hd = d // N_HEADS
    n_pages = page_table.shape[1]
    assert bp == 1 and ss == 1 and sp % BLOCK == 0 and d == N_HEADS * hd
    assert n_pages * PAGE_SIZE % BLOCK == 0
    assert cache_k.shape[2:] == (PAGE_SIZE, N_HEADS, hd) and CONV_W == 3

    xp = x_prompt.reshape(sp, d)
    xs = x_sample.reshape(bs, d)
    ck, cv = cache_k, cache_v
    pt_flat = page_table.reshape(-1)
    g_final = norm_final.reshape(1, d)
    bf = lambda w: w.astype(BF16)

    kp_l, vp_l, ks_l, vs_l, cp_l, cs_l = [], [], [], [], [], []
    for i in range(depth):
        g_mix = norm_mix[i].reshape(1, d)
        g_ffn = norm_ffn[i].reshape(1, d)
        last = i == depth - 1
        if i % 2 == 0:
            a = i // 2
            wqkv = bf(w_qkv[a])
            kp, vp, qt, kb, va, km = _qkv_prompt(xp, g_mix, wqkv)
            ap, ksum = _attn_prompt(qt, kb, va, km.reshape(sp // BLOCK, d), pt_flat, ck, a,
                                    bs, n_pages)
            qkv_s = _norm_proj(xs, g_mix, wqkv)
            qs, kn, vn = qkv_s[:, :d], qkv_s[:, d:2 * d], qkv_s[:, 2 * d:]
            sel = _select_sample(ksum, qs)[:, :, :, 0].reshape(-1)
            as_ = _attn_sample(pt_flat, sel, qs, kn, vn, ck, cv, a, n_pages)
            w_out = bf(w_o[a])
            kp_l.append(kp.reshape(bp, sp, N_HEADS, hd))
            vp_l.append(vp.reshape(bp, sp, N_HEADS, hd))
            ks_l.append(kn.reshape(bs, ss, N_HEADS, hd))
            vs_l.append(vn.reshape(bs, ss, N_HEADS, hd))
        else:
            c = i // 2
            win = bf(w_conv_in[c])
            zero_state = jnp.zeros((CONV_W - 1, d), F32)
            ap, cp = _conv_prompt(xp, g_mix, win, conv_w[c], zero_state)
            s0, s1 = state_conv[c, :, 0, :], state_conv[c, :, 1, :]
            as_, zs = _conv_sample(xs, g_mix, win, conv_w[c], s0, s1)
            w_out = bf(w_conv_out[c])
            cp_l.append(cp.reshape(bp, CONV_W - 1, d))
            cs_l.append(jnp.stack([s1, zs], axis=1))
        wup, wdn = bf(w_up[i]), bf(w_down[i])
        xp = _post(ap, xp, w_out, g_ffn, wup, wdn, g_final, last, tm=512)
        xs = _post(as_, xs, w_out, g_ffn, wup, wdn, g_final, last, tm=bs)
    return (xp.reshape(bp, sp, d), xs.reshape(bs, ss, d), jnp.stack(kp_l), jnp.stack(vp_l),
            jnp.stack(ks_l), jnp.stack(vs_l), jnp.stack(cp_l), jnp.stack(cs_l))
```

```python
import functools

import jax
import jax.numpy as jnp
from jax import lax
from jax.experimental import pallas as pl
from jax.experimental.pallas import tpu as pltpu

F32 = jnp.float32
BF16 = jnp.bfloat16

N_HEADS = 8
BLOCK = 256
TOP_K = 3
PAGE_SIZE = 128
CONV_W = 3
EPS = 1e-6
LOG2E = 1.4426950408889634
GROUP = 4
Q_BLOCKS = 2
ONES_ROWS = 16

VMEM_LIMIT = 56 * 1024 * 1024


def _params(*sem):
    return pltpu.CompilerParams(dimension_semantics=sem, vmem_limit_bytes=VMEM_LIMIT)


def _rmsnorm(x, g):
    y = x * lax.rsqrt(jnp.mean(x * x, axis=-1, keepdims=True) + EPS)
    return y * g


def _top_k_rows(gate, n_valid):
    r = gate.shape[0]
    row = lax.broadcasted_iota(jnp.int32, gate.shape, 0).astype(F32)
    valid = row < n_valid
    g = jnp.where(valid, gate, -jnp.inf)
    sel = jnp.zeros(gate.shape, jnp.bool_)
    picks = []
    for _ in range(TOP_K):
        m = jnp.max(g, axis=0, keepdims=True)
        idx = jnp.min(jnp.where(g == m, row, float(r)), axis=0, keepdims=True)
        hit = row == idx
        sel = sel | hit
        g = jnp.where(hit, -jnp.inf, g)
        picks.append(idx)
    return sel & valid, jnp.concatenate(picks, axis=0)


def _qkv_prompt_kernel(x_ref, g_ref, w_ref, k_ref, v_ref, qt_ref, kb_ref, vt_ref, km_ref):
    d = x_ref.shape[1]
    tm = x_ref.shape[0]
    h = _rmsnorm(x_ref[...], g_ref[...]).astype(BF16)
    q = jnp.dot(h, w_ref[:, 0:d], preferred_element_type=F32)
    qt_ref[...] = q.T.astype(BF16)
    k = jnp.dot(h, w_ref[:, d:2 * d], preferred_element_type=F32)
    k_ref[...] = k
    kb_ref[...] = k.astype(BF16)
    km_ref[...] = jnp.sum(k.reshape(tm // BLOCK, BLOCK, d), axis=1) / BLOCK
    v = jnp.dot(h, w_ref[:, 2 * d:3 * d], preferred_element_type=F32)
    v_ref[...] = v
    vt = v.T.astype(BF16)
    hd = d // N_HEADS
    for hh in range(N_HEADS):
        vt_ref[hh, 0:hd, :] = vt[hh * hd:(hh + 1) * hd, :]
        vt_ref[hh, hd:, :] = jnp.ones((ONES_ROWS, tm), BF16)


def _qkv_prompt(x, g, w_bf, tm=512):
    s, d = x.shape
    nt = s // tm
    hd = d // N_HEADS
    row = pl.BlockSpec((tm, d), lambda i: (i, 0))
    col = pl.BlockSpec((d, tm), lambda i: (0, i))
    return pl.pallas_call(
        _qkv_prompt_kernel,
        grid=(nt,),
        in_specs=[row, pl.BlockSpec((1, d), lambda i: (0, 0)),
                  pl.BlockSpec((d, 3 * d), lambda i: (0, 0))],
        out_specs=[row, row, col, row,
                   pl.BlockSpec((N_HEADS, hd + ONES_ROWS, tm), lambda i: (0, 0, i)),
                   pl.BlockSpec((None, tm // BLOCK, d), lambda i: (i, 0, 0))],
        out_shape=[jax.ShapeDtypeStruct((s, d), F32), jax.ShapeDtypeStruct((s, d), F32),
                   jax.ShapeDtypeStruct((d, s), BF16), jax.ShapeDtypeStruct((s, d), BF16),
                   jax.ShapeDtypeStruct((N_HEADS, hd + ONES_ROWS, s), BF16),
                   jax.ShapeDtypeStruct((nt, tm // BLOCK, d), F32)],
        compiler_params=_params("parallel"),
        name="qkv_prompt",
    )(x, g, w_bf)


def _proj_kernel(x_ref, g_ref, w_ref, o_ref):
    h = _rmsnorm(x_ref[...], g_ref[...]).astype(BF16)
    o_ref[...] = jnp.dot(h, w_ref[...], preferred_element_type=F32)


def _norm_proj(x, g, w_bf, tn=1024):
    m, d = x.shape
    n = w_bf.shape[1]
    return pl.pallas_call(
        _proj_kernel,
        grid=(n // tn,),
        in_specs=[pl.BlockSpec((m, d), lambda j: (0, 0)), pl.BlockSpec((1, d), lambda j: (0, 0)),
                  pl.BlockSpec((d, tn), lambda j: (0, j))],
        out_specs=pl.BlockSpec((m, tn), lambda j: (0, j)),
        out_shape=jax.ShapeDtypeStruct((m, n), F32),
        compiler_params=_params("parallel"),
        name="norm_proj_sample",
    )(x, g, w_bf)


def _attn_prompt_kernel(pt_ref, qt_ref, k_ref, va_ref, km_ref, cache_ref, o_ref, ksum_ref,
                        bias_ref, acc_ref, sa_ref, sb_ref, pa_ref, pb_ref, pg_ref, pg_sem,
                        *, scale, layer):
    step = pl.program_id(0) * pl.num_programs(1) + pl.program_id(1)
    n_steps = pl.num_programs(0) * pl.num_programs(1)
    pps = pg_ref.shape[1]
    ppb = BLOCK // PAGE_SIZE
    slot = step % 2

    def page_copies(st, sl):
        return [pltpu.make_async_copy(cache_ref.at[layer, pt_ref[st * pps + r]],
                                      pg_ref.at[sl, r], pg_sem.at[sl, r]) for r in range(pps)]

    @pl.when(step == 0)
    def _():
        for cp in page_copies(step, slot):
            cp.start()

    ahead = jnp.minimum(step + 1, n_steps - 1)
    for cp in page_copies(ahead, 1 - slot):
        cp.start()
    for cp in page_copies(step, slot):
        cp.wait()
    for b in range(pps // ppb):
        acc = jnp.sum(pg_ref[slot, b * ppb], axis=0)
        for r in range(1, ppb):
            acc = acc + jnp.sum(pg_ref[slot, b * ppb + r], axis=0)
        ksum_ref[b] = acc

    nb = km_ref.shape[0]
    hd, tq = qt_ref.shape
    b0 = pl.program_id(1) * Q_BLOCKS
    gk = GROUP * BLOCK
    n_groups = nb // GROUP
    c = scale * LOG2E
    qtb = qt_ref[...]
    km = km_ref[...]
    km_hi = km.astype(BF16)
    km_lo = (km - km_hi.astype(F32)).astype(BF16)
    gate = (jnp.dot(km_hi, qtb, preferred_element_type=F32)
            + jnp.dot(km_lo, qtb, preferred_element_type=F32))
    qcol = lax.broadcasted_iota(jnp.int32, (1, tq), 1)
    own = b0 + qcol // BLOCK
    sel, _ = _top_k_rows(gate, own.astype(F32))
    picked = jnp.where(sel, 0.0, -jnp.inf)
    blk = lax.broadcasted_iota(jnp.int32, (nb, tq), 0)
    bias_ref[...] = jnp.where(blk < b0, picked, -jnp.inf)

    def scores(g, s_ref):
        off = pl.multiple_of(g * gk, gk)
        s_ref[...] = jnp.dot(k_ref[pl.ds(off, gk), :], qtb, preferred_element_type=F32)
        return jnp.concatenate(
            [jnp.max(s_ref[b * BLOCK:(b + 1) * BLOCK, :], axis=0, keepdims=True)
             for b in range(GROUP)], axis=0)

    def weights(g, s_ref, cm, p_ref, m):
        bias = [bias_ref[pl.ds(g * GROUP + b, 1), :] for b in range(GROUP)]
        m_new = m
        for b in range(GROUP):
            m_new = jnp.maximum(m_new, cm[b:b + 1, :] + bias[b])
        for b in range(GROUP):
            shift = jnp.where(bias[b] == 0.0, m_new, jnp.inf)
            rows = slice(b * BLOCK, (b + 1) * BLOCK)
            p_ref[rows, :] = jnp.exp2((s_ref[rows, :] - shift) * c).astype(BF16)
        return m_new, jnp.exp2((m - m_new) * c)

    def values(g, p_ref, alpha):
        off = pl.multiple_of(g * gk, gk)
        acc_ref[...] = alpha * acc_ref[...] + jnp.dot(
            va_ref[:, pl.ds(off, gk)], p_ref[...], preferred_element_type=F32)

    off_own = pl.multiple_of(b0 * BLOCK, tq)
    s_own = jnp.dot(k_ref[pl.ds(off_own, tq), :], qtb, preferred_element_type=F32)
    key = lax.broadcasted_iota(jnp.int32, s_own.shape, 0)
    qry = lax.broadcasted_iota(jnp.int32, s_own.shape, 1)
    picked_first = jnp.max(jnp.where(blk == b0, picked, -jnp.inf), axis=0, keepdims=True)
    first_ok = jnp.where(qcol < BLOCK, 0.0, picked_first)
    s_own = jnp.where(key <= qry, s_own, -jnp.inf) + jnp.where(key < BLOCK, first_ok, 0.0)
    m0 = jnp.max(s_own, axis=0, keepdims=True)
    p_own = jnp.exp2((s_own - m0) * c).astype(BF16)
    acc_ref[...] = jnp.dot(va_ref[:, pl.ds(off_own, tq)], p_own, preferred_element_type=F32)

    pb_ref[...] = jnp.zeros_like(pb_ref)
    n_pairs = (b0 + 2 * GROUP - 1) // (2 * GROUP)
    cm_a0 = scores(0, sa_ref)

    def pair(t, carry):
        m, alpha_b, cm_a = carry
        g0 = 2 * t
        cm_b = scores(g0 + 1, sb_ref)
        m, alpha_a = weights(g0, sa_ref, cm_a, pa_ref, m)
        values(jnp.maximum(g0 - 1, 0), pb_ref, alpha_b)
        cm_a = scores(jnp.minimum(g0 + 2, n_groups - 1), sa_ref)
        m, alpha_b = weights(g0 + 1, sb_ref, cm_b, pb_ref, m)
        values(g0, pa_ref, alpha_a)
        return m, alpha_b, cm_a

    m, alpha_b, _ = lax.fori_loop(0, n_pairs, pair, (m0, jnp.ones((1, tq), F32), cm_a0))
    values(jnp.maximum(2 * n_pairs - 1, 0), pb_ref, alpha_b)
    acc = acc_ref[...]
    o_ref[...] = (acc[0:hd, :] / acc[hd:hd + 1, :]).T.astype(o_ref.dtype)

    @pl.when(step == n_steps - 1)
    def _():
        for cp in page_copies(ahead, 1 - slot):
            cp.wait()


def _attn_prompt(qt, kb, va, km, page_table_flat, cache, layer, n_seq, n_pages):
    d, s = qt.shape
    hd = d // N_HEADS
    nb = s // BLOCK
    assert nb % (2 * GROUP) == 0 and Q_BLOCKS == 2
    gk = GROUP * BLOCK
    tq = Q_BLOCKS * BLOCK
    nt = nb // Q_BLOCKS
    steps = N_HEADS * nt
    ppb = BLOCK // PAGE_SIZE
    pps = n_seq * n_pages // steps
    assert pps * steps == n_seq * n_pages and pps % ppb == 0 and n_pages % pps == 0
    steps_per_seq = n_pages // pps

    ksum_spec = pl.BlockSpec(
        (None, pps // ppb, N_HEADS, hd),
        lambda h, i, pt: ((h * nt + i) // steps_per_seq, (h * nt + i) % steps_per_seq, 0, 0))
    return pl.pallas_call(
        functools.partial(_attn_prompt_kernel, scale=hd ** -0.5, layer=layer),
        grid_spec=pltpu.PrefetchScalarGridSpec(
            num_scalar_prefetch=1,
            grid=(N_HEADS, nt),
            in_specs=[pl.BlockSpec((hd, tq), lambda h, i, pt: (h, i)),
                      pl.BlockSpec((s, hd), lambda h, i, pt: (0, h)),
                      pl.BlockSpec((None, hd + ONES_ROWS, s), lambda h, i, pt: (h, 0, 0)),
                      pl.BlockSpec((nb, hd), lambda h, i, pt: (0, h)),
                      pl.BlockSpec(memory_space=pl.ANY)],
            out_specs=[pl.BlockSpec((tq, hd), lambda h, i, pt: (i, h)), ksum_spec],
            scratch_shapes=[pltpu.VMEM((nb, tq), F32),
                            pltpu.VMEM((hd + ONES_ROWS, tq), F32),
                            pltpu.VMEM((gk, tq), F32), pltpu.VMEM((gk, tq), F32),
                            pltpu.VMEM((gk, tq), BF16), pltpu.VMEM((gk, tq), BF16),
                            pltpu.VMEM((2, pps, PAGE_SIZE, N_HEADS, hd), F32),
                            pltpu.SemaphoreType.DMA((2, pps))],
        ),
        out_shape=[jax.ShapeDtypeStruct((s, d), BF16),
                   jax.ShapeDtypeStruct((n_seq, n_pages // ppb, N_HEADS, hd), F32)],
        compiler_params=_params("arbitrary", "arbitrary"),
        name="attn_prompt",
    )(page_table_flat, qt, kb, va, km, cache)


def _post_kernel(a_ref, x_ref, wo_ref, g_ref, wup_ref, wdn_ref, gf_ref, out_ref,
                 *, final_norm, tf):
    ff = wup_ref.shape[1]
    y = x_ref[...] + jnp.dot(a_ref[...], wo_ref[...], preferred_element_type=F32)
    h = _rmsnorm(y, g_ref[...]).astype(BF16)
    for f in range(ff // tf):
        u = jnp.dot(h, wup_ref[:, f * tf:(f + 1) * tf], preferred_element_type=F32)
        u = jnp.maximum(u, 0.0)
        y = y + jnp.dot((u * u).astype(BF16), wdn_ref[f * tf:(f + 1) * tf, :],
                        preferred_element_type=F32)
    if final_norm:
        y = _rmsnorm(y, gf_ref[...])
    out_ref[...] = y


def _post(a_bf, x, wo_bf, g_ffn, wup_bf, wdn_bf, layer, g_final, final_norm, tm, tf=1024):
    s, d = x.shape
    ff = wup_bf.shape[2]
    tm = min(tm, s)
    row = lambda i: (i, 0)
    fix = lambda i: (0, 0)
    lyr = lambda i: (layer, 0, 0)
    once = pl.Buffered(1)
    return pl.pallas_call(
        functools.partial(_post_kernel, final_norm=final_norm, tf=tf),
        grid=(s // tm,),
        in_specs=[pl.BlockSpec((tm, d), row), pl.BlockSpec((tm, d), row),
                  pl.BlockSpec((d, d), fix, pipeline_mode=once), pl.BlockSpec((1, d), fix),
                  pl.BlockSpec((None, d, ff), lyr, pipeline_mode=once),
                  pl.BlockSpec((None, ff, d), lyr, pipeline_mode=once),
                  pl.BlockSpec((1, d), fix)],
        out_specs=pl.BlockSpec((tm, d), row),
        out_shape=jax.ShapeDtypeStruct((s, d), F32),
        compiler_params=_params("parallel"),
        name="post_mlp",
    )(a_bf, x, wo_bf, g_ffn, wup_bf, wdn_bf, g_final)


def _conv_prompt_kernel(x_ref, g_ref, w_ref, cw_ref, st_ref, a_ref, so_ref, carry_ref):
    tm, d = x_ref.shape

    @pl.when(pl.program_id(0) == 0)
    def _():
        carry_ref[...] = st_ref[...]

    h = _rmsnorm(x_ref[...], g_ref[...]).astype(BF16)
    gc = jnp.dot(h, w_ref[:, d:2 * d], preferred_element_type=F32)
    u = jnp.dot(h, w_ref[:, 2 * d:3 * d], preferred_element_type=F32)
    z = gc * u
    prev2 = carry_ref[0:1, :]
    prev1 = carry_ref[1:2, :]
    row = lax.broadcasted_iota(jnp.int32, (tm, d), 0)
    z1 = jnp.where(row == 0, prev1, pltpu.roll(z, 1, 0))
    z2 = jnp.where(row == 0, prev2, jnp.where(row == 1, prev1, pltpu.roll(z, 2, 0)))
    y = cw_ref[0:1, :] * z2
    y = y + cw_ref[1:2, :] * z1
    y = y + cw_ref[2:3, :] * z
    gb = jnp.dot(h, w_ref[:, 0:d], preferred_element_type=F32)
    a_ref[...] = (gb * y).astype(a_ref.dtype)
    tail = z[tm - (CONV_W - 1):tm, :]
    carry_ref[...] = tail
    so_ref[...] = tail


def _conv_prompt(x, g, w_bf, cw, state, tm=512):
    s, d = x.shape
    fix = lambda i: (0, 0)
    return pl.pallas_call(
        _conv_prompt_kernel,
        grid=(s // tm,),
        in_specs=[pl.BlockSpec((tm, d), lambda i: (i, 0)), pl.BlockSpec((1, d), fix),
                  pl.BlockSpec((d, 3 * d), fix), pl.BlockSpec((CONV_W, d), fix),
                  pl.BlockSpec((CONV_W - 1, d), fix)],
        out_specs=[pl.BlockSpec((tm, d), lambda i: (i, 0)), pl.BlockSpec((CONV_W - 1, d), fix)],
        out_shape=[jax.ShapeDtypeStruct((s, d), BF16), jax.ShapeDtypeStruct((CONV_W - 1, d), F32)],
        scratch_shapes=[pltpu.VMEM((CONV_W - 1, d), F32)],
        compiler_params=_params("arbitrary"),
        name="conv_prompt",
    )(x, g, w_bf, cw, state)


def _conv_sample_kernel(x_ref, g_ref, w_ref, cw_ref, s0_ref, s1_ref, a_ref, z_ref):
    d = x_ref.shape[1]
    h = _rmsnorm(x_ref[...], g_ref[...]).astype(BF16)
    gb = jnp.dot(h, w_ref[:, 0:d], preferred_element_type=F32)
    gc = jnp.dot(h, w_ref[:, d:2 * d], preferred_element_type=F32)
    u = jnp.dot(h, w_ref[:, 2 * d:3 * d], preferred_element_type=F32)
    z = gc * u
    y = cw_ref[0:1, :] * s0_ref[...]
    y = y + cw_ref[1:2, :] * s1_ref[...]
    y = y + cw_ref[2:3, :] * z
    a_ref[...] = (gb * y).astype(a_ref.dtype)
    z_ref[...] = z


def _conv_sample(x, g, w_bf, cw, s0, s1):
    m, d = x.shape
    return pl.pallas_call(
        _conv_sample_kernel,
        out_shape=[jax.ShapeDtypeStruct((m, d), BF16), jax.ShapeDtypeStruct((m, d), F32)],
        compiler_params=pltpu.CompilerParams(vmem_limit_bytes=VMEM_LIMIT),
        name="conv_sample",
    )(x, g, w_bf, cw, s0, s1)


def _select_sample_kernel(ks_ref, q_ref, sel_ref):
    nb = ks_ref.shape[0]
    prod = (ks_ref[...] / BLOCK) * q_ref[...]
    gate = jnp.sum(prod, axis=-1, keepdims=True)
    _, picks = _top_k_rows(gate, float(nb))
    sel_ref[...] = jnp.broadcast_to(picks, sel_ref.shape).astype(jnp.int32)


def _select_sample(ksum, q):
    n_seq, nb, nh, hd = ksum.shape
    return pl.pallas_call(
        _select_sample_kernel,
        grid=(n_seq,),
        in_specs=[pl.BlockSpec((None, nb, nh, hd), lambda b: (b, 0, 0, 0)),
                  pl.BlockSpec((None, nh, hd), lambda b: (b, 0, 0))],
        out_specs=pl.BlockSpec((None, TOP_K, nh, hd), lambda b: (b, 0, 0, 0)),
        out_shape=jax.ShapeDtypeStruct((n_seq, TOP_K, nh, hd), jnp.int32),
        compiler_params=_params("parallel"),
        name="select_sample",
    )(ksum, q.reshape(n_seq, nh, hd))


PAGES_PER_HEAD = TOP_K * (BLOCK // PAGE_SIZE)


def _attn_sample_kernel(pt_ref, sel_ref, q_ref, kn_ref, vn_ref, ck_ref, cv_ref, o_ref,
                        kbuf, vbuf, sem, *, scale, layer, n_pages):
    b = pl.program_id(0)
    n_seq = pl.num_programs(0)
    nh, hd = q_ref.shape
    ppb = BLOCK // PAGE_SIZE
    c = scale * LOG2E

    def copies(seq, slot):
        out = []
        for h in range(nh):
            for s in range(TOP_K):
                blk = sel_ref[(seq * TOP_K + s) * nh + h]
                for r in range(ppb):
                    page = pt_ref[seq * n_pages + blk * ppb + r]
                    i = h * PAGES_PER_HEAD + s * ppb + r
                    out.append(pltpu.make_async_copy(ck_ref.at[layer, page, :, h, :],
                                                     kbuf.at[slot, i], sem.at[slot, 0, i]))
                    out.append(pltpu.make_async_copy(cv_ref.at[layer, page, :, h, :],
                                                     vbuf.at[slot, i], sem.at[slot, 1, i]))
        return out

    slot = b % 2

    @pl.when(b == 0)
    def _():
        for cp in copies(b, slot):
            cp.start()

    @pl.when(b + 1 < n_seq)
    def _():
        for cp in copies(b + 1, 1 - slot):
            cp.start()

    for cp in copies(b, slot):
        cp.wait()

    for h in range(nh):
        qb = jnp.broadcast_to(q_ref[h:h + 1, :], (8, hd)).astype(BF16)
        pages = range(h * PAGES_PER_HEAD, (h + 1) * PAGES_PER_HEAD)
        logits = [lax.dot_general(qb, kbuf[slot, i].astype(BF16), (((1,), (1,)), ((), ())),
                                  preferred_element_type=F32) for i in pages]
        knb = kn_ref[h:h + 1, :].astype(BF16).astype(F32)
        own = jnp.sum(qb.astype(F32) * knb, axis=-1, keepdims=True)
        m = own
        for lg in logits:
            m = jnp.maximum(m, jnp.max(lg, axis=-1, keepdims=True))
        p_own = jnp.exp2((own - m) * c)
        l = p_own
        acc = p_own.astype(BF16).astype(F32) * vn_ref[h:h + 1, :].astype(BF16).astype(F32)
        for lg, i in zip(logits, pages):
            p = jnp.exp2((lg - m) * c)
            l = l + jnp.sum(p, axis=-1, keepdims=True)
            acc = acc + jnp.dot(p.astype(BF16), vbuf[slot, i].astype(BF16),
                                preferred_element_type=F32)
        o_ref[h:h + 1, :] = (acc / l)[0:1, :]


def _attn_sample(page_table_flat, sel_flat, q, kn, vn, cache_k, cache_v, layer, n_pages):
    n_seq, d = q.shape
    hd = d // N_HEADS
    vec = pl.BlockSpec((None, N_HEADS, hd), lambda b, pt, sel: (b, 0, 0))
    to3 = lambda a: a.reshape(n_seq, N_HEADS, hd)
    n_buf = N_HEADS * PAGES_PER_HEAD
    out = pl.pallas_call(
        functools.partial(_attn_sample_kernel, scale=hd ** -0.5, layer=layer, n_pages=n_pages),
        grid_spec=pltpu.PrefetchScalarGridSpec(
            num_scalar_prefetch=2,
            grid=(n_seq,),
            in_specs=[vec, vec, vec, pl.BlockSpec(memory_space=pl.ANY),
                      pl.BlockSpec(memory_space=pl.ANY)],
            out_specs=vec,
            scratch_shapes=[pltpu.VMEM((2, n_buf, PAGE_SIZE, hd), F32),
                            pltpu.VMEM((2, n_buf, PAGE_SIZE, hd), F32),
                            pltpu.SemaphoreType.DMA((2, 2, n_buf))],
        ),
        out_shape=jax.ShapeDtypeStruct((n_seq, N_HEADS, hd), F32),
        compiler_params=_params("arbitrary"),
        name="attn_sample",
    )(page_table_flat, sel_flat, to3(q), to3(kn), to3(vn), cache_k, cache_v)
    return out.reshape(n_seq, d).astype(BF16)


def kernel(x_prompt, x_sample, cache_k, cache_v, state_conv, page_table, norm_mix, norm_ffn,
           norm_final, w_qkv, w_o, w_conv_in, conv_w, w_conv_out, w_up, w_down):
    bp, sp, d = x_prompt.shape
    bs, ss, _ = x_sample.shape
    depth = norm_mix.shape[0]
    hd = d // N_HEADS
    n_pages = page_table.shape[1]
    assert bp == 1 and ss == 1 and sp % BLOCK == 0 and d == N_HEADS * hd
    assert n_pages * PAGE_SIZE % BLOCK == 0
    assert cache_k.shape[2:] == (PAGE_SIZE, N_HEADS, hd) and CONV_W == 3

    xp = x_prompt.reshape(sp, d)
    xs = x_sample.reshape(bs, d)
    ck, cv = cache_k, cache_v
    pt_flat = page_table.reshape(-1)
    g_final = norm_final.reshape(1, d)
    bf = lambda w: w.astype(BF16)
    wup_all, wdn_all = bf(w_up), bf(w_down)

    kp_l, vp_l, ks_l, vs_l, cp_l, cs_l = [], [], [], [], [], []
    for i in range(depth):
        g_mix = norm_mix[i].reshape(1, d)
        g_ffn = norm_ffn[i].reshape(1, d)
        last = i == depth - 1
        if i % 2 == 0:
            a = i // 2
            wqkv = bf(w_qkv[a])
            kp, vp, qt, kb, va, km = _qkv_prompt(xp, g_mix, wqkv)
            ap, ksum = _attn_prompt(qt, kb, va, km.reshape(sp // BLOCK, d), pt_flat, ck, a,
                                    bs, n_pages)
            qkv_s = _norm_proj(xs, g_mix, wqkv)
            qs, kn, vn = qkv_s[:, :d], qkv_s[:, d:2 * d], qkv_s[:, 2 * d:]
            sel = _select_sample(ksum, qs)[:, :, :, 0].reshape(-1)
            as_ = _attn_sample(pt_flat, sel, qs, kn, vn, ck, cv, a, n_pages)
            w_out = bf(w_o[a])
            kp_l.append(kp.reshape(bp, sp, N_HEADS, hd))
            vp_l.append(vp.reshape(bp, sp, N_HEADS, hd))
            ks_l.append(kn.reshape(bs, ss, N_HEADS, hd))
            vs_l.append(vn.reshape(bs, ss, N_HEADS, hd))
        else:
            c = i // 2
            win = bf(w_conv_in[c])
            zero_state = jnp.zeros((CONV_W - 1, d), F32)
            ap, cp = _conv_prompt(xp, g_mix, win, conv_w[c], zero_state)
            s0, s1 = state_conv[c, :, 0, :], state_conv[c, :, 1, :]
            as_, zs = _conv_sample(xs, g_mix, win, conv_w[c], s0, s1)
            w_out = bf(w_conv_out[c])
            cp_l.append(cp.reshape(bp, CONV_W - 1, d))
            cs_l.append(jnp.stack([s1, zs], axis=1))
        xp = _post(ap, xp, w_out, g_ffn, wup_all, wdn_all, i, g_final, last, tm=512)
        xs = _post(as_, xs, w_out, g_ffn, wup_all, wdn_all, i, g_final, last, tm=bs)
    return (xp.reshape(bp, sp, d), xs.reshape(bs, ss, d), jnp.stack(kp_l), jnp.stack(vp_l),
            jnp.stack(ks_l), jnp.stack(vs_l), jnp.stack(cp_l), jnp.stack(cs_l))
```

```python
import functools

import jax
import jax.numpy as jnp
from jax import lax
from jax.experimental import pallas as pl
from jax.experimental.pallas import tpu as pltpu

F32 = jnp.float32
BF16 = jnp.bfloat16

N_HEADS = 8
BLOCK = 256
TOP_K = 3
PAGE_SIZE = 128
CONV_W = 3
EPS = 1e-6
LOG2E = 1.4426950408889634
GROUP = 4
Q_BLOCKS = 2
ONES_ROWS = 16

VMEM_LIMIT = 56 * 1024 * 1024


def _params(*sem):
    return pltpu.CompilerParams(dimension_semantics=sem, vmem_limit_bytes=VMEM_LIMIT)


def _rmsnorm(x, g):
    y = x * lax.rsqrt(jnp.mean(x * x, axis=-1, keepdims=True) + EPS)
    return y * g


def _top_k_rows(gate, n_valid):
    r = gate.shape[0]
    row = lax.broadcasted_iota(jnp.int32, gate.shape, 0).astype(F32)
    valid = row < n_valid
    g = jnp.where(valid, gate, -jnp.inf)
    sel = jnp.zeros(gate.shape, jnp.bool_)
    picks = []
    for _ in range(TOP_K):
        m = jnp.max(g, axis=0, keepdims=True)
        idx = jnp.min(jnp.where(g == m, row, float(r)), axis=0, keepdims=True)
        hit = row == idx
        sel = sel | hit
        g = jnp.where(hit, -jnp.inf, g)
        picks.append(idx)
    return sel & valid, jnp.concatenate(picks, axis=0)


def _qkv_prompt_kernel(x_ref, g_ref, w_ref, k_ref, v_ref, qt_ref, kb_ref, vt_ref, km_ref):
    d = x_ref.shape[1]
    tm = x_ref.shape[0]
    h = _rmsnorm(x_ref[...], g_ref[...]).astype(BF16)
    q = jnp.dot(h, w_ref[:, 0:d], preferred_element_type=F32)
    qt_ref[...] = q.T.astype(BF16)
    k = jnp.dot(h, w_ref[:, d:2 * d], preferred_element_type=F32)
    k_ref[...] = k
    kb_ref[...] = k.astype(BF16)
    km_ref[...] = jnp.sum(k.reshape(tm // BLOCK, BLOCK, d), axis=1) / BLOCK
    v = jnp.dot(h, w_ref[:, 2 * d:3 * d], preferred_element_type=F32)
    v_ref[...] = v
    vt = v.T.astype(BF16)
    hd = d // N_HEADS
    for hh in range(N_HEADS):
        vt_ref[hh, 0:hd, :] = vt[hh * hd:(hh + 1) * hd, :]
        vt_ref[hh, hd:, :] = jnp.ones((ONES_ROWS, tm), BF16)


def _qkv_prompt(x, g, w_bf, tm=512):
    s, d = x.shape
    nt = s // tm
    hd = d // N_HEADS
    row = pl.BlockSpec((tm, d), lambda i: (i, 0))
    col = pl.BlockSpec((d, tm), lambda i: (0, i))
    return pl.pallas_call(
        _qkv_prompt_kernel,
        grid=(nt,),
        in_specs=[row, pl.BlockSpec((1, d), lambda i: (0, 0)),
                  pl.BlockSpec((d, 3 * d), lambda i: (0, 0))],
        out_specs=[row, row, col, row,
                   pl.BlockSpec((N_HEADS, hd + ONES_ROWS, tm), lambda i: (0, 0, i)),
                   pl.BlockSpec((None, tm // BLOCK, d), lambda i: (i, 0, 0))],
        out_shape=[jax.ShapeDtypeStruct((s, d), F32), jax.ShapeDtypeStruct((s, d), F32),
                   jax.ShapeDtypeStruct((d, s), BF16), jax.ShapeDtypeStruct((s, d), BF16),
                   jax.ShapeDtypeStruct((N_HEADS, hd + ONES_ROWS, s), BF16),
                   jax.ShapeDtypeStruct((nt, tm // BLOCK, d), F32)],
        compiler_params=_params("parallel"),
        name="qkv_prompt",
    )(x, g, w_bf)


def _proj_kernel(x_ref, g_ref, w_ref, o_ref):
    h = _rmsnorm(x_ref[...], g_ref[...]).astype(BF16)
    o_ref[...] = jnp.dot(h, w_ref[...], preferred_element_type=F32)


def _norm_proj(x, g, w_bf, tn=1024):
    m, d = x.shape
    n = w_bf.shape[1]
    return pl.pallas_call(
        _proj_kernel,
        grid=(n // tn,),
        in_specs=[pl.BlockSpec((m, d), lambda j: (0, 0)), pl.BlockSpec((1, d), lambda j: (0, 0)),
                  pl.BlockSpec((d, tn), lambda j: (0, j))],
        out_specs=pl.BlockSpec((m, tn), lambda j: (0, j)),
        out_shape=jax.ShapeDtypeStruct((m, n), F32),
        compiler_params=_params("parallel"),
        name="norm_proj_sample",
    )(x, g, w_bf)


def _attn_prompt_kernel(pt_ref, qt_ref, k_ref, va_ref, km_ref, cache_ref, o_ref, ksum_ref,
                        bias_ref, acc_ref, sa_ref, sb_ref, pa_ref, pb_ref, pg_ref, pg_sem,
                        *, scale, layer):
    step = pl.program_id(0) * pl.num_programs(1) + pl.program_id(1)
    n_steps = pl.num_programs(0) * pl.num_programs(1)
    pps = pg_ref.shape[1]
    ppb = BLOCK // PAGE_SIZE
    slot = step % 2

    def page_copies(st, sl):
        return [pltpu.make_async_copy(cache_ref.at[layer, pt_ref[st * pps + r]],
                                      pg_ref.at[sl, r], pg_sem.at[sl, r]) for r in range(pps)]

    @pl.when(step == 0)
    def _():
        for cp in page_copies(step, slot):
            cp.start()

    ahead = jnp.minimum(step + 1, n_steps - 1)
    for cp in page_copies(ahead, 1 - slot):
        cp.start()
    for cp in page_copies(step, slot):
        cp.wait()
    for b in range(pps // ppb):
        acc = jnp.sum(pg_ref[slot, b * ppb], axis=0)
        for r in range(1, ppb):
            acc = acc + jnp.sum(pg_ref[slot, b * ppb + r], axis=0)
        ksum_ref[b] = acc

    nb = km_ref.shape[0]
    hd, tq = qt_ref.shape
    b0 = pl.program_id(1) * Q_BLOCKS
    gk = GROUP * BLOCK
    n_groups = nb // GROUP
    c = scale * LOG2E
    qtb = qt_ref[...]
    km = km_ref[...]
    km_hi = km.astype(BF16)
    km_lo = (km - km_hi.astype(F32)).astype(BF16)
    gate = (jnp.dot(km_hi, qtb, preferred_element_type=F32)
            + jnp.dot(km_lo, qtb, preferred_element_type=F32))
    qcol = lax.broadcasted_iota(jnp.int32, (1, tq), 1)
    own = b0 + qcol // BLOCK
    sel, _ = _top_k_rows(gate, own.astype(F32))
    picked = jnp.where(sel, 0.0, -jnp.inf)
    blk = lax.broadcasted_iota(jnp.int32, (nb, tq), 0)
    bias_ref[...] = jnp.where(blk < b0, picked, -jnp.inf)

    def scores(g, s_ref):
        off = pl.multiple_of(g * gk, gk)
        s_ref[...] = jnp.dot(k_ref[pl.ds(off, gk), :], qtb, preferred_element_type=F32)
        return jnp.concatenate(
            [jnp.max(s_ref[b * BLOCK:(b + 1) * BLOCK, :], axis=0, keepdims=True)
             for b in range(GROUP)], axis=0)

    def weights(g, s_ref, cm, p_ref, m):
        bias = [bias_ref[pl.ds(g * GROUP + b, 1), :] for b in range(GROUP)]
        m_new = m
        for b in range(GROUP):
            m_new = jnp.maximum(m_new, cm[b:b + 1, :] + bias[b])
        for b in range(GROUP):
            shift = jnp.where(bias[b] == 0.0, m_new, jnp.inf)
            rows = slice(b * BLOCK, (b + 1) * BLOCK)
            p_ref[rows, :] = jnp.exp2((s_ref[rows, :] - shift) * c).astype(BF16)
        return m_new, jnp.exp2((m - m_new) * c)

    def values(g, p_ref, alpha):
        off = pl.multiple_of(g * gk, gk)
        acc_ref[...] = alpha * acc_ref[...] + jnp.dot(
            va_ref[:, pl.ds(off, gk)], p_ref[...], preferred_element_type=F32)

    off_own = pl.multiple_of(b0 * BLOCK, tq)
    s_own = jnp.dot(k_ref[pl.ds(off_own, tq), :], qtb, preferred_element_type=F32)
    key = lax.broadcasted_iota(jnp.int32, s_own.shape, 0)
    qry = lax.broadcasted_iota(jnp.int32, s_own.shape, 1)
    picked_first = jnp.max(jnp.where(blk == b0, picked, -jnp.inf), axis=0, keepdims=True)
    first_ok = jnp.where(qcol < BLOCK, 0.0, picked_first)
    s_own = jnp.where(key <= qry, s_own, -jnp.inf) + jnp.where(key < BLOCK, first_ok, 0.0)
    m0 = jnp.max(s_own, axis=0, keepdims=True)
    p_own = jnp.exp2((s_own - m0) * c).astype(BF16)
    acc_ref[...] = jnp.dot(va_ref[:, pl.ds(off_own, tq)], p_own, preferred_element_type=F32)

    pb_ref[...] = jnp.zeros_like(pb_ref)
    n_pairs = (b0 + 2 * GROUP - 1) // (2 * GROUP)
    cm_a0 = scores(0, sa_ref)

    def pair(t, carry):
        m, alpha_b, cm_a = carry
        g0 = 2 * t
        cm_b = scores(g0 + 1, sb_ref)
        m, alpha_a = weights(g0, sa_ref, cm_a, pa_ref, m)
        values(jnp.maximum(g0 - 1, 0), pb_ref, alpha_b)
        cm_a = scores(jnp.minimum(g0 + 2, n_groups - 1), sa_ref)
        m, alpha_b = weights(g0 + 1, sb_ref, cm_b, pb_ref, m)
        values(g0, pa_ref, alpha_a)
        return m, alpha_b, cm_a

    m, alpha_b, _ = lax.fori_loop(0, n_pairs, pair, (m0, jnp.ones((1, tq), F32), cm_a0))
    values(jnp.maximum(2 * n_pairs - 1, 0), pb_ref, alpha_b)
    acc = acc_ref[...]
    o_ref[...] = (acc[0:hd, :] / acc[hd:hd + 1, :]).T.astype(o_ref.dtype)

    @pl.when(step == n_steps - 1)
    def _():
        for cp in page_copies(ahead, 1 - slot):
            cp.wait()


def _attn_prompt(qt, kb, va, km, page_table_flat, cache, layer, n_seq, n_pages):
    d, s = qt.shape
    hd = d // N_HEADS
    nb = s // BLOCK
    assert nb % (2 * GROUP) == 0 and Q_BLOCKS == 2
    gk = GROUP * BLOCK
    tq = Q_BLOCKS * BLOCK
    nt = nb // Q_BLOCKS
    steps = N_HEADS * nt
    ppb = BLOCK // PAGE_SIZE
    pps = n_seq * n_pages // steps
    assert pps * steps == n_seq * n_pages and pps % ppb == 0 and n_pages % pps == 0
    steps_per_seq = n_pages // pps

    ksum_spec = pl.BlockSpec(
        (None, pps // ppb, N_HEADS, hd),
        lambda h, i, pt: ((h * nt + i) // steps_per_seq, (h * nt + i) % steps_per_seq, 0, 0))
    return pl.pallas_call(
        functools.partial(_attn_prompt_kernel, scale=hd ** -0.5, layer=layer),
        grid_spec=pltpu.PrefetchScalarGridSpec(
            num_scalar_prefetch=1,
            grid=(N_HEADS, nt),
            in_specs=[pl.BlockSpec((hd, tq), lambda h, i, pt: (h, i)),
                      pl.BlockSpec((s, hd), lambda h, i, pt: (0, h)),
                      pl.BlockSpec((None, hd + ONES_ROWS, s), lambda h, i, pt: (h, 0, 0)),
                      pl.BlockSpec((nb, hd), lambda h, i, pt: (0, h)),
                      pl.BlockSpec(memory_space=pl.ANY)],
            out_specs=[pl.BlockSpec((tq, hd), lambda h, i, pt: (i, h)), ksum_spec],
            scratch_shapes=[pltpu.VMEM((nb, tq), F32),
                            pltpu.VMEM((hd + ONES_ROWS, tq), F32),
                            pltpu.VMEM((gk, tq), F32), pltpu.VMEM((gk, tq), F32),
                            pltpu.VMEM((gk, tq), BF16), pltpu.VMEM((gk, tq), BF16),
                            pltpu.VMEM((2, pps, PAGE_SIZE, N_HEADS, hd), F32),
                            pltpu.SemaphoreType.DMA((2, pps))],
        ),
        out_shape=[jax.ShapeDtypeStruct((s, d), BF16),
                   jax.ShapeDtypeStruct((n_seq, n_pages // ppb, N_HEADS, hd), F32)],
        compiler_params=_params("arbitrary", "arbitrary"),
        name="attn_prompt",
    )(page_table_flat, qt, kb, va, km, cache)


def _post_kernel(a_ref, x_ref, as_ref, xs_ref, wo_ref, g_ref, wup_ref, wdn_ref, gf_ref,
                 out_ref, outs_ref, *, final_norm, tf):
    ff = wup_ref.shape[1]

    def rows(a, x):
        y = x + jnp.dot(a, wo_ref[...], preferred_element_type=F32)
        h = _rmsnorm(y, g_ref[...]).astype(BF16)
        for f in range(ff // tf):
            u = jnp.dot(h, wup_ref[:, f * tf:(f + 1) * tf], preferred_element_type=F32)
            u = jnp.maximum(u, 0.0)
            y = y + jnp.dot((u * u).astype(BF16), wdn_ref[f * tf:(f + 1) * tf, :],
                            preferred_element_type=F32)
        if final_norm:
            y = _rmsnorm(y, gf_ref[...])
        return y

    out_ref[...] = rows(a_ref[...], x_ref[...])

    @pl.when(pl.program_id(0) == pl.num_programs(0) - 1)
    def _():
        outs_ref[...] = rows(as_ref[...], xs_ref[...])


def _post(a_bf, x, as_bf, xs, wo_bf, g_ffn, wup_bf, wdn_bf, g_final, final_norm, tm=512,
          tf=1024):
    s, d = x.shape
    ms = xs.shape[0]
    ff = wup_bf.shape[1]
    row = lambda i: (i, 0)
    fix = lambda i: (0, 0)
    once = pl.Buffered(1)
    return pl.pallas_call(
        functools.partial(_post_kernel, final_norm=final_norm, tf=tf),
        grid=(s // tm,),
        in_specs=[pl.BlockSpec((tm, d), row), pl.BlockSpec((tm, d), row),
                  pl.BlockSpec((ms, d), fix), pl.BlockSpec((ms, d), fix),
                  pl.BlockSpec((d, d), fix, pipeline_mode=once), pl.BlockSpec((1, d), fix),
                  pl.BlockSpec((d, ff), fix, pipeline_mode=once),
                  pl.BlockSpec((ff, d), fix, pipeline_mode=once),
                  pl.BlockSpec((1, d), fix)],
        out_specs=[pl.BlockSpec((tm, d), row), pl.BlockSpec((ms, d), fix)],
        out_shape=[jax.ShapeDtypeStruct((s, d), F32), jax.ShapeDtypeStruct((ms, d), F32)],
        compiler_params=_params("arbitrary"),
        name="post_mlp",
    )(a_bf, x, as_bf, xs, wo_bf, g_ffn, wup_bf, wdn_bf, g_final)


def _conv_prompt_kernel(x_ref, g_ref, w_ref, cw_ref, st_ref, a_ref, so_ref, carry_ref):
    tm, d = x_ref.shape

    @pl.when(pl.program_id(0) == 0)
    def _():
        carry_ref[...] = st_ref[...]

    h = _rmsnorm(x_ref[...], g_ref[...]).astype(BF16)
    gc = jnp.dot(h, w_ref[:, d:2 * d], preferred_element_type=F32)
    u = jnp.dot(h, w_ref[:, 2 * d:3 * d], preferred_element_type=F32)
    z = gc * u
    prev2 = carry_ref[0:1, :]
    prev1 = carry_ref[1:2, :]
    row = lax.broadcasted_iota(jnp.int32, (tm, d), 0)
    z1 = jnp.where(row == 0, prev1, pltpu.roll(z, 1, 0))
    z2 = jnp.where(row == 0, prev2, jnp.where(row == 1, prev1, pltpu.roll(z, 2, 0)))
    y = cw_ref[0:1, :] * z2
    y = y + cw_ref[1:2, :] * z1
    y = y + cw_ref[2:3, :] * z
    gb = jnp.dot(h, w_ref[:, 0:d], preferred_element_type=F32)
    a_ref[...] = (gb * y).astype(a_ref.dtype)
    tail = z[tm - (CONV_W - 1):tm, :]
    carry_ref[...] = tail
    so_ref[...] = tail


def _conv_prompt(x, g, w_bf, cw, state, tm=512):
    s, d = x.shape
    fix = lambda i: (0, 0)
    return pl.pallas_call(
        _conv_prompt_kernel,
        grid=(s // tm,),
        in_specs=[pl.BlockSpec((tm, d), lambda i: (i, 0)), pl.BlockSpec((1, d), fix),
                  pl.BlockSpec((d, 3 * d), fix), pl.BlockSpec((CONV_W, d), fix),
                  pl.BlockSpec((CONV_W - 1, d), fix)],
        out_specs=[pl.BlockSpec((tm, d), lambda i: (i, 0)), pl.BlockSpec((CONV_W - 1, d), fix)],
        out_shape=[jax.ShapeDtypeStruct((s, d), BF16), jax.ShapeDtypeStruct((CONV_W - 1, d), F32)],
        scratch_shapes=[pltpu.VMEM((CONV_W - 1, d), F32)],
        compiler_params=_params("arbitrary"),
        name="conv_prompt",
    )(x, g, w_bf, cw, state)


def _conv_sample_kernel(x_ref, g_ref, w_ref, cw_ref, s0_ref, s1_ref, a_ref, z_ref):
    d = x_ref.shape[1]
    h = _rmsnorm(x_ref[...], g_ref[...]).astype(BF16)
    gb = jnp.dot(h, w_ref[:, 0:d], preferred_element_type=F32)
    gc = jnp.dot(h, w_ref[:, d:2 * d], preferred_element_type=F32)
    u = jnp.dot(h, w_ref[:, 2 * d:3 * d], preferred_element_type=F32)
    z = gc * u
    y = cw_ref[0:1, :] * s0_ref[...]
    y = y + cw_ref[1:2, :] * s1_ref[...]
    y = y + cw_ref[2:3, :] * z
    a_ref[...] = (gb * y).astype(a_ref.dtype)
    z_ref[...] = z


def _conv_sample(x, g, w_bf, cw, s0, s1):
    m, d = x.shape
    return pl.pallas_call(
        _conv_sample_kernel,
        out_shape=[jax.ShapeDtypeStruct((m, d), BF16), jax.ShapeDtypeStruct((m, d), F32)],
        compiler_params=pltpu.CompilerParams(vmem_limit_bytes=VMEM_LIMIT),
        name="conv_sample",
    )(x, g, w_bf, cw, s0, s1)


def _select_sample_kernel(ks_ref, q_ref, sel_ref):
    nb = ks_ref.shape[0]
    prod = (ks_ref[...] / BLOCK) * q_ref[...]
    gate = jnp.sum(prod, axis=-1, keepdims=True)
    _, picks = _top_k_rows(gate, float(nb))
    sel_ref[...] = jnp.broadcast_to(picks, sel_ref.shape).astype(jnp.int32)


def _select_sample(ksum, q):
    n_seq, nb, nh, hd = ksum.shape
    return pl.pallas_call(
        _select_sample_kernel,
        grid=(n_seq,),
        in_specs=[pl.BlockSpec((None, nb, nh, hd), lambda b: (b, 0, 0, 0)),
                  pl.BlockSpec((None, nh, hd), lambda b: (b, 0, 0))],
        out_specs=pl.BlockSpec((None, TOP_K, nh, hd), lambda b: (b, 0, 0, 0)),
        out_shape=jax.ShapeDtypeStruct((n_seq, TOP_K, nh, hd), jnp.int32),
        compiler_params=_params("parallel"),
        name="select_sample",
    )(ksum, q.reshape(n_seq, nh, hd))


PAGES_PER_HEAD = TOP_K * (BLOCK // PAGE_SIZE)


def _attn_sample_kernel(pt_ref, sel_ref, q_ref, kn_ref, vn_ref, ck_ref, cv_ref, o_ref,
                        kbuf, vbuf, sem, *, scale, layer, n_pages):
    b = pl.program_id(0)
    n_seq = pl.num_programs(0)
    nh, hd = q_ref.shape
    ppb = BLOCK // PAGE_SIZE
    c = scale * LOG2E

    def copies(seq, slot):
        out = []
        for h in range(nh):
            for s in range(TOP_K):
                blk = sel_ref[(seq * TOP_K + s) * nh + h]
                for r in range(ppb):
                    page = pt_ref[seq * n_pages + blk * ppb + r]
                    i = h * PAGES_PER_HEAD + s * ppb + r
                    out.append(pltpu.make_async_copy(ck_ref.at[layer, page, :, h, :],
                                                     kbuf.at[slot, i], sem.at[slot, 0, i]))
                    out.append(pltpu.make_async_copy(cv_ref.at[layer, page, :, h, :],
                                                     vbuf.at[slot, i], sem.at[slot, 1, i]))
        return out

    slot = b % 2

    @pl.when(b == 0)
    def _():
        for cp in copies(b, slot):
            cp.start()

    @pl.when(b + 1 < n_seq)
    def _():
        for cp in copies(b + 1, 1 - slot):
            cp.start()

    for cp in copies(b, slot):
        cp.wait()

    for h in range(nh):
        qb = jnp.broadcast_to(q_ref[h:h + 1, :], (8, hd)).astype(BF16)
        pages = range(h * PAGES_PER_HEAD, (h + 1) * PAGES_PER_HEAD)
        logits = [lax.dot_general(qb, kbuf[slot, i].astype(BF16), (((1,), (1,)), ((), ())),
                                  preferred_element_type=F32) for i in pages]
        knb = kn_ref[h:h + 1, :].astype(BF16).astype(F32)
        own = jnp.sum(qb.astype(F32) * knb, axis=-1, keepdims=True)
        m = own
        for lg in logits:
            m = jnp.maximum(m, jnp.max(lg, axis=-1, keepdims=True))
        p_own = jnp.exp2((own - m) * c)
        l = p_own
        acc = p_own.astype(BF16).astype(F32) * vn_ref[h:h + 1, :].astype(BF16).astype(F32)
        for lg, i in zip(logits, pages):
            p = jnp.exp2((lg - m) * c)
            l = l + jnp.sum(p, axis=-1, keepdims=True)
            acc = acc + jnp.dot(p.astype(BF16), vbuf[slot, i].astype(BF16),
                                preferred_element_type=F32)
        o_ref[h:h + 1, :] = (acc / l)[0:1, :]


def _attn_sample(page_table_flat, sel_flat, q, kn, vn, cache_k, cache_v, layer, n_pages):
    n_seq, d = q.shape
    hd = d // N_HEADS
    vec = pl.BlockSpec((None, N_HEADS, hd), lambda b, pt, sel: (b, 0, 0))
    to3 = lambda a: a.reshape(n_seq, N_HEADS, hd)
    n_buf = N_HEADS * PAGES_PER_HEAD
    out = pl.pallas_call(
        functools.partial(_attn_sample_kernel, scale=hd ** -0.5, layer=layer, n_pages=n_pages),
        grid_spec=pltpu.PrefetchScalarGridSpec(
            num_scalar_prefetch=2,
            grid=(n_seq,),
            in_specs=[vec, vec, vec, pl.BlockSpec(memory_space=pl.ANY),
                      pl.BlockSpec(memory_space=pl.ANY)],
            out_specs=vec,
            scratch_shapes=[pltpu.VMEM((2, n_buf, PAGE_SIZE, hd), F32),
                            pltpu.VMEM((2, n_buf, PAGE_SIZE, hd), F32),
                            pltpu.SemaphoreType.DMA((2, 2, n_buf))],
        ),
        out_shape=jax.ShapeDtypeStruct((n_seq, N_HEADS, hd), F32),
        compiler_params=_params("arbitrary"),
        name="attn_sample",
    )(page_table_flat, sel_flat, to3(q), to3(kn), to3(vn), cache_k, cache_v)
    return out.reshape(n_seq, d).astype(BF16)


def kernel(x_prompt, x_sample, cache_k, cache_v, state_conv, page_table, norm_mix, norm_ffn,
           norm_final, w_qkv, w_o, w_conv_in, conv_w, w_conv_out, w_up, w_down):
    bp, sp, d = x_prompt.shape
    bs, ss, _ = x_sample.shape
    depth = norm_mix.shape[0]
    hd = d // N_HEADS
    n_pages = page_table.shape[1]
    assert bp == 1 and ss == 1 and sp % BLOCK == 0 and d == N_HEADS * hd
    assert n_pages * PAGE_SIZE % BLOCK == 0
    assert cache_k.shape[2:] == (PAGE_SIZE, N_HEADS, hd) and CONV_W == 3

    xp = x_prompt.reshape(sp, d)
    xs = x_sample.reshape(bs, d)
    ck, cv = cache_k, cache_v
    pt_flat = page_table.reshape(-1)
    g_final = norm_final.reshape(1, d)
    bf = lambda w: w.astype(BF16)

    kp_l, vp_l, ks_l, vs_l, cp_l, cs_l = [], [], [], [], [], []
    for i in range(depth):
        g_mix = norm_mix[i].reshape(1, d)
        g_ffn = norm_ffn[i].reshape(1, d)
        last = i == depth - 1
        if i % 2 == 0:
            a = i // 2
            wqkv = bf(w_qkv[a])
            kp, vp, qt, kb, va, km = _qkv_prompt(xp, g_mix, wqkv)
            ap, ksum = _attn_prompt(qt, kb, va, km.reshape(sp // BLOCK, d), pt_flat, ck, a,
                                    bs, n_pages)
            qkv_s = _norm_proj(xs, g_mix, wqkv)
            qs, kn, vn = qkv_s[:, :d], qkv_s[:, d:2 * d], qkv_s[:, 2 * d:]
            sel = _select_sample(ksum, qs)[:, :, :, 0].reshape(-1)
            as_ = _attn_sample(pt_flat, sel, qs, kn, vn, ck, cv, a, n_pages)
            w_out = bf(w_o[a])
            kp_l.append(kp.reshape(bp, sp, N_HEADS, hd))
            vp_l.append(vp.reshape(bp, sp, N_HEADS, hd))
            ks_l.append(kn.reshape(bs, ss, N_HEADS, hd))
            vs_l.append(vn.reshape(bs, ss, N_HEADS, hd))
        else:
            c = i // 2
            win = bf(w_conv_in[c])
            zero_state = jnp.zeros((CONV_W - 1, d), F32)
            ap, cp = _conv_prompt(xp, g_mix, win, conv_w[c], zero_state)
            s0, s1 = state_conv[c, :, 0, :], state_conv[c, :, 1, :]
            as_, zs = _conv_sample(xs, g_mix, win, conv_w[c], s0, s1)
            w_out = bf(w_conv_out[c])
            cp_l.append(cp.reshape(bp, CONV_W - 1, d))
            cs_l.append(jnp.stack([s1, zs], axis=1))
        wup, wdn = bf(w_up[i]), bf(w_down[i])
        xp, xs = _post(ap, xp, as_, xs, w_out, g_ffn, wup, wdn, g_final, last)
    return (xp.reshape(bp, sp, d), xs.reshape(bs, ss, d), jnp.stack(kp_l), jnp.stack(vp_l),
            jnp.stack(ks_l), jnp.stack(vs_l), jnp.stack(cp_l), jnp.stack(cs_l))
```

```python
import functools

import jax
import jax.numpy as jnp
from jax import lax
from jax.experimental import pallas as pl
from jax.experimental.pallas import tpu as pltpu

F32 = jnp.float32
BF16 = jnp.bfloat16

N_HEADS = 8
BLOCK = 256
TOP_K = 3
PAGE_SIZE = 128
CONV_W = 3
EPS = 1e-6
LOG2E = 1.4426950408889634
GROUP = 4
Q_BLOCKS = 2
ONES_ROWS = 16

VMEM_LIMIT = 56 * 1024 * 1024


def _params(*sem):
    return pltpu.CompilerParams(dimension_semantics=sem, vmem_limit_bytes=VMEM_LIMIT)


def _rmsnorm(x, g):
    y = x * lax.rsqrt(jnp.mean(x * x, axis=-1, keepdims=True) + EPS)
    return y * g


def _top_k_rows(gate, n_valid):
    r = gate.shape[0]
    row = lax.broadcasted_iota(jnp.int32, gate.shape, 0).astype(F32)
    valid = row < n_valid
    g = jnp.where(valid, gate, -jnp.inf)
    sel = jnp.zeros(gate.shape, jnp.bool_)
    picks = []
    for _ in range(TOP_K):
        m = jnp.max(g, axis=0, keepdims=True)
        idx = jnp.min(jnp.where(g == m, row, float(r)), axis=0, keepdims=True)
        hit = row == idx
        sel = sel | hit
        g = jnp.where(hit, -jnp.inf, g)
        picks.append(idx)
    return sel & valid, jnp.concatenate(picks, axis=0)


def _qkv_prompt_kernel(x_ref, g_ref, w_ref, k_ref, v_ref, qt_ref, kb_ref, vt_ref, km_ref):
    d = x_ref.shape[1]
    tm = x_ref.shape[0]
    h = _rmsnorm(x_ref[...], g_ref[...]).astype(BF16)
    q = jnp.dot(h, w_ref[:, 0:d], preferred_element_type=F32)
    qt_ref[...] = q.T.astype(BF16)
    k = jnp.dot(h, w_ref[:, d:2 * d], preferred_element_type=F32)
    k_ref[...] = k
    kb_ref[...] = k.astype(BF16)
    km_ref[...] = jnp.sum(k.reshape(tm // BLOCK, BLOCK, d), axis=1) / BLOCK
    v = jnp.dot(h, w_ref[:, 2 * d:3 * d], preferred_element_type=F32)
    v_ref[...] = v
    vt = v.T.astype(BF16)
    hd = d // N_HEADS
    for hh in range(N_HEADS):
        vt_ref[hh, 0:hd, :] = vt[hh * hd:(hh + 1) * hd, :]
        vt_ref[hh, hd:, :] = jnp.ones((ONES_ROWS, tm), BF16)


def _qkv_prompt(x, g, w_bf, tm=512):
    s, d = x.shape
    nt = s // tm
    hd = d // N_HEADS
    row = pl.BlockSpec((tm, d), lambda i: (i, 0))
    col = pl.BlockSpec((d, tm), lambda i: (0, i))
    return pl.pallas_call(
        _qkv_prompt_kernel,
        grid=(nt,),
        in_specs=[row, pl.BlockSpec((1, d), lambda i: (0, 0)),
                  pl.BlockSpec((d, 3 * d), lambda i: (0, 0))],
        out_specs=[row, row, col, row,
                   pl.BlockSpec((N_HEADS, hd + ONES_ROWS, tm), lambda i: (0, 0, i)),
                   pl.BlockSpec((None, tm // BLOCK, d), lambda i: (i, 0, 0))],
        out_shape=[jax.ShapeDtypeStruct((s, d), F32), jax.ShapeDtypeStruct((s, d), F32),
                   jax.ShapeDtypeStruct((d, s), BF16), jax.ShapeDtypeStruct((s, d), BF16),
                   jax.ShapeDtypeStruct((N_HEADS, hd + ONES_ROWS, s), BF16),
                   jax.ShapeDtypeStruct((nt, tm // BLOCK, d), F32)],
        compiler_params=_params("parallel"),
        name="qkv_prompt",
    )(x, g, w_bf)


def _proj_kernel(x_ref, g_ref, w_ref, o_ref):
    h = _rmsnorm(x_ref[...], g_ref[...]).astype(BF16)
    o_ref[...] = jnp.dot(h, w_ref[...], preferred_element_type=F32)


def _norm_proj(x, g, w_bf, tn=1024):
    m, d = x.shape
    n = w_bf.shape[1]
    return pl.pallas_call(
        _proj_kernel,
        grid=(n // tn,),
        in_specs=[pl.BlockSpec((m, d), lambda j: (0, 0)), pl.BlockSpec((1, d), lambda j: (0, 0)),
                  pl.BlockSpec((d, tn), lambda j: (0, j))],
        out_specs=pl.BlockSpec((m, tn), lambda j: (0, j)),
        out_shape=jax.ShapeDtypeStruct((m, n), F32),
        compiler_params=_params("parallel"),
        name="norm_proj_sample",
    )(x, g, w_bf)


def _attn_prompt_kernel(pt_ref, qt_ref, k_ref, va_ref, km_ref, cache_ref, o_ref, ksum_ref,
                        bias_ref, acc_ref, sa_ref, sb_ref, pa_ref, pb_ref, pg_ref, pg_sem,
                        *, scale, layer):
    step = pl.program_id(0) * pl.num_programs(1) + pl.program_id(1)
    n_steps = pl.num_programs(0) * pl.num_programs(1)
    pps = pg_ref.shape[1]
    ppb = BLOCK // PAGE_SIZE
    slot = step % 2

    def page_copies(st, sl):
        return [pltpu.make_async_copy(cache_ref.at[layer, pt_ref[st * pps + r]],
                                      pg_ref.at[sl, r], pg_sem.at[sl, r]) for r in range(pps)]

    @pl.when(step == 0)
    def _():
        for cp in page_copies(step, slot):
            cp.start()

    ahead = jnp.minimum(step + 1, n_steps - 1)
    for cp in page_copies(ahead, 1 - slot):
        cp.start()
    for cp in page_copies(step, slot):
        cp.wait()
    for b in range(pps // ppb):
        acc = jnp.sum(pg_ref[slot, b * ppb], axis=0)
        for r in range(1, ppb):
            acc = acc + jnp.sum(pg_ref[slot, b * ppb + r], axis=0)
        ksum_ref[b] = acc

    nb = km_ref.shape[0]
    hd, tq = qt_ref.shape
    b0 = pl.program_id(1) * Q_BLOCKS
    gk = GROUP * BLOCK
    n_groups = nb // GROUP
    c = scale * LOG2E
    qtb = qt_ref[...]
    km = km_ref[...]
    km_hi = km.astype(BF16)
    km_lo = (km - km_hi.astype(F32)).astype(BF16)
    gate = (jnp.dot(km_hi, qtb, preferred_element_type=F32)
            + jnp.dot(km_lo, qtb, preferred_element_type=F32))
    qcol = lax.broadcasted_iota(jnp.int32, (1, tq), 1)
    own = b0 + qcol // BLOCK
    sel, _ = _top_k_rows(gate, own.astype(F32))
    picked = jnp.where(sel, 0.0, -jnp.inf)
    blk = lax.broadcasted_iota(jnp.int32, (nb, tq), 0)
    bias_ref[...] = jnp.where(blk < b0, picked, -jnp.inf)

    def scores(g, s_ref):
        off = pl.multiple_of(g * gk, gk)
        s_ref[...] = jnp.dot(k_ref[pl.ds(off, gk), :], qtb, preferred_element_type=F32)
        return jnp.concatenate(
            [jnp.max(s_ref[b * BLOCK:(b + 1) * BLOCK, :], axis=0, keepdims=True)
             for b in range(GROUP)], axis=0)

    def weights(g, s_ref, cm, p_ref, m):
        bias = [bias_ref[pl.ds(g * GROUP + b, 1), :] for b in range(GROUP)]
        m_new = m
        for b in range(GROUP):
            m_new = jnp.maximum(m_new, cm[b:b + 1, :] + bias[b])
        for b in range(GROUP):
            shift = jnp.where(bias[b] == 0.0, m_new, jnp.inf)
            rows = slice(b * BLOCK, (b + 1) * BLOCK)
            p_ref[rows, :] = jnp.exp2((s_ref[rows, :] - shift) * c).astype(BF16)
        return m_new, jnp.exp2((m - m_new) * c)

    def values(g, p_ref, alpha):
        off = pl.multiple_of(g * gk, gk)
        acc_ref[...] = alpha * acc_ref[...] + jnp.dot(
            va_ref[:, pl.ds(off, gk)], p_ref[...], preferred_element_type=F32)

    off_own = pl.multiple_of(b0 * BLOCK, tq)
    s_own = jnp.dot(k_ref[pl.ds(off_own, tq), :], qtb, preferred_element_type=F32)
    key = lax.broadcasted_iota(jnp.int32, s_own.shape, 0)
    qry = lax.broadcasted_iota(jnp.int32, s_own.shape, 1)
    picked_first = jnp.max(jnp.where(blk == b0, picked, -jnp.inf), axis=0, keepdims=True)
    first_ok = jnp.where(qcol < BLOCK, 0.0, picked_first)
    s_own = jnp.where(key <= qry, s_own, -jnp.inf) + jnp.where(key < BLOCK, first_ok, 0.0)
    m0 = jnp.max(s_own, axis=0, keepdims=True)
    p_own = jnp.exp2((s_own - m0) * c).astype(BF16)
    acc_ref[...] = jnp.dot(va_ref[:, pl.ds(off_own, tq)], p_own, preferred_element_type=F32)

    pb_ref[...] = jnp.zeros_like(pb_ref)
    n_pairs = (b0 + 2 * GROUP - 1) // (2 * GROUP)
    cm_a0 = scores(0, sa_ref)

    def pair(t, carry):
        m, alpha_b, cm_a = carry
        g0 = 2 * t
        cm_b = scores(g0 + 1, sb_ref)
        m, alpha_a = weights(g0, sa_ref, cm_a, pa_ref, m)
        values(jnp.maximum(g0 - 1, 0), pb_ref, alpha_b)
        cm_a = scores(jnp.minimum(g0 + 2, n_groups - 1), sa_ref)
        m, alpha_b = weights(g0 + 1, sb_ref, cm_b, pb_ref, m)
        values(g0, pa_ref, alpha_a)
        return m, alpha_b, cm_a

    m, alpha_b, _ = lax.fori_loop(0, n_pairs, pair, (m0, jnp.ones((1, tq), F32), cm_a0))
    values(jnp.maximum(2 * n_pairs - 1, 0), pb_ref, alpha_b)
    acc = acc_ref[...]
    o_ref[...] = (acc[0:hd, :] / acc[hd:hd + 1, :]).T.astype(o_ref.dtype)

    @pl.when(step == n_steps - 1)
    def _():
        for cp in page_copies(ahead, 1 - slot):
            cp.wait()


def _attn_prompt(qt, kb, va, km, page_table_flat, cache, layer, n_seq, n_pages):
    d, s = qt.shape
    hd = d // N_HEADS
    nb = s // BLOCK
    assert nb % (2 * GROUP) == 0 and Q_BLOCKS == 2
    gk = GROUP * BLOCK
    tq = Q_BLOCKS * BLOCK
    nt = nb // Q_BLOCKS
    steps = N_HEADS * nt
    ppb = BLOCK // PAGE_SIZE
    pps = n_seq * n_pages // steps
    assert pps * steps == n_seq * n_pages and pps % ppb == 0 and n_pages % pps == 0
    steps_per_seq = n_pages // pps

    ksum_spec = pl.BlockSpec(
        (None, pps // ppb, N_HEADS, hd),
        lambda h, i, pt: ((h * nt + i) // steps_per_seq, (h * nt + i) % steps_per_seq, 0, 0))
    return pl.pallas_call(
        functools.partial(_attn_prompt_kernel, scale=hd ** -0.5, layer=layer),
        grid_spec=pltpu.PrefetchScalarGridSpec(
            num_scalar_prefetch=1,
            grid=(N_HEADS, nt),
            in_specs=[pl.BlockSpec((hd, tq), lambda h, i, pt: (h, i)),
                      pl.BlockSpec((s, hd), lambda h, i, pt: (0, h)),
                      pl.BlockSpec((None, hd + ONES_ROWS, s), lambda h, i, pt: (h, 0, 0)),
                      pl.BlockSpec((nb, hd), lambda h, i, pt: (0, h)),
                      pl.BlockSpec(memory_space=pl.ANY)],
            out_specs=[pl.BlockSpec((tq, hd), lambda h, i, pt: (i, h)), ksum_spec],
            scratch_shapes=[pltpu.VMEM((nb, tq), F32),
                            pltpu.VMEM((hd + ONES_ROWS, tq), F32),
                            pltpu.VMEM((gk, tq), F32), pltpu.VMEM((gk, tq), F32),
                            pltpu.VMEM((gk, tq), BF16), pltpu.VMEM((gk, tq), BF16),
                            pltpu.VMEM((2, pps, PAGE_SIZE, N_HEADS, hd), F32),
                            pltpu.SemaphoreType.DMA((2, pps))],
        ),
        out_shape=[jax.ShapeDtypeStruct((s, d), BF16),
                   jax.ShapeDtypeStruct((n_seq, n_pages // ppb, N_HEADS, hd), F32)],
        compiler_params=_params("arbitrary", "arbitrary"),
        name="attn_prompt",
    )(page_table_flat, qt, kb, va, km, cache)


def _post_kernel(a_ref, x_ref, as_ref, xs_ref, wo_ref, g_ref, wup_ref, wdn_ref, gf_ref,
                 out_ref, outs_ref, *, final_norm, tf):
    ff = wup_ref.shape[1]

    def rows(a, x):
        y = x + jnp.dot(a, wo_ref[...], preferred_element_type=F32)
        h = _rmsnorm(y, g_ref[...]).astype(BF16)
        for f in range(ff // tf):
            u = jnp.dot(h, wup_ref[:, f * tf:(f + 1) * tf], preferred_element_type=F32)
            u = jnp.maximum(u, 0.0)
            y = y + jnp.dot((u * u).astype(BF16), wdn_ref[f * tf:(f + 1) * tf, :],
                            preferred_element_type=F32)
        if final_norm:
            y = _rmsnorm(y, gf_ref[...])
        return y

    out_ref[...] = rows(a_ref[...], x_ref[...])

    @pl.when(pl.program_id(0) == pl.num_programs(0) - 1)
    def _():
        outs_ref[...] = rows(as_ref[...], xs_ref[...])


def _post(a_bf, x, as_bf, xs, wo_bf, g_ffn, wup_bf, wdn_bf, g_final, final_norm, tm=512,
          tf=1024):
    s, d = x.shape
    ms = xs.shape[0]
    ff = wup_bf.shape[1]
    row = lambda i: (i, 0)
    fix = lambda i: (0, 0)
    once = pl.Buffered(1)
    return pl.pallas_call(
        functools.partial(_post_kernel, final_norm=final_norm, tf=tf),
        grid=(s // tm,),
        in_specs=[pl.BlockSpec((tm, d), row), pl.BlockSpec((tm, d), row),
                  pl.BlockSpec((ms, d), fix), pl.BlockSpec((ms, d), fix),
                  pl.BlockSpec((d, d), fix, pipeline_mode=once), pl.BlockSpec((1, d), fix),
                  pl.BlockSpec((d, ff), fix, pipeline_mode=once),
                  pl.BlockSpec((ff, d), fix, pipeline_mode=once),
                  pl.BlockSpec((1, d), fix)],
        out_specs=[pl.BlockSpec((tm, d), row), pl.BlockSpec((ms, d), fix)],
        out_shape=[jax.ShapeDtypeStruct((s, d), F32), jax.ShapeDtypeStruct((ms, d), F32)],
        compiler_params=_params("arbitrary"),
        name="post_mlp",
    )(a_bf, x, as_bf, xs, wo_bf, g_ffn, wup_bf, wdn_bf, g_final)


def _conv_prompt_kernel(x_ref, g_ref, w_ref, cw_ref, st_ref, a_ref, so_ref, carry_ref):
    tm, d = x_ref.shape

    @pl.when(pl.program_id(0) == 0)
    def _():
        carry_ref[...] = st_ref[...]

    h = _rmsnorm(x_ref[...], g_ref[...]).astype(BF16)
    gc = jnp.dot(h, w_ref[:, d:2 * d], preferred_element_type=F32)
    u = jnp.dot(h, w_ref[:, 2 * d:3 * d], preferred_element_type=F32)
    z = gc * u
    prev2 = carry_ref[0:1, :]
    prev1 = carry_ref[1:2, :]
    row = lax.broadcasted_iota(jnp.int32, (tm, d), 0)
    z1 = jnp.where(row == 0, prev1, pltpu.roll(z, 1, 0))
    z2 = jnp.where(row == 0, prev2, jnp.where(row == 1, prev1, pltpu.roll(z, 2, 0)))
    y = cw_ref[0:1, :] * z2
    y = y + cw_ref[1:2, :] * z1
    y = y + cw_ref[2:3, :] * z
    gb = jnp.dot(h, w_ref[:, 0:d], preferred_element_type=F32)
    a_ref[...] = (gb * y).astype(a_ref.dtype)
    tail = z[tm - (CONV_W - 1):tm, :]
    carry_ref[...] = tail
    so_ref[...] = tail


def _conv_prompt(x, g, w_bf, cw, state, tm=512):
    s, d = x.shape
    fix = lambda i: (0, 0)
    return pl.pallas_call(
        _conv_prompt_kernel,
        grid=(s // tm,),
        in_specs=[pl.BlockSpec((tm, d), lambda i: (i, 0)), pl.BlockSpec((1, d), fix),
                  pl.BlockSpec((d, 3 * d), fix), pl.BlockSpec((CONV_W, d), fix),
                  pl.BlockSpec((CONV_W - 1, d), fix)],
        out_specs=[pl.BlockSpec((tm, d), lambda i: (i, 0)), pl.BlockSpec((CONV_W - 1, d), fix)],
        out_shape=[jax.ShapeDtypeStruct((s, d), BF16), jax.ShapeDtypeStruct((CONV_W - 1, d), F32)],
        scratch_shapes=[pltpu.VMEM((CONV_W - 1, d), F32)],
        compiler_params=_params("arbitrary"),
        name="conv_prompt",
    )(x, g, w_bf, cw, state)


def _conv_sample_kernel(x_ref, g_ref, w_ref, cw_ref, s0_ref, s1_ref, a_ref, z_ref):
    d = x_ref.shape[1]
    h = _rmsnorm(x_ref[...], g_ref[...]).astype(BF16)
    gb = jnp.dot(h, w_ref[:, 0:d], preferred_element_type=F32)
    gc = jnp.dot(h, w_ref[:, d:2 * d], preferred_element_type=F32)
    u = jnp.dot(h, w_ref[:, 2 * d:3 * d], preferred_element_type=F32)
    z = gc * u
    y = cw_ref[0:1, :] * s0_ref[...]
    y = y + cw_ref[1:2, :] * s1_ref[...]
    y = y + cw_ref[2:3, :] * z
    a_ref[...] = (gb * y).astype(a_ref.dtype)
    z_ref[...] = z


def _conv_sample(x, g, w_bf, cw, s0, s1):
    m, d = x.shape
    return pl.pallas_call(
        _conv_sample_kernel,
        out_shape=[jax.ShapeDtypeStruct((m, d), BF16), jax.ShapeDtypeStruct((m, d), F32)],
        compiler_params=pltpu.CompilerParams(vmem_limit_bytes=VMEM_LIMIT),
        name="conv_sample",
    )(x, g, w_bf, cw, s0, s1)


def _select_sample_kernel(ks_ref, q_ref, sel_ref):
    nb = ks_ref.shape[0]
    prod = (ks_ref[...] / BLOCK) * q_ref[...]
    gate = jnp.sum(prod, axis=-1, keepdims=True)
    _, picks = _top_k_rows(gate, float(nb))
    sel_ref[...] = jnp.broadcast_to(picks, sel_ref.shape).astype(jnp.int32)


def _select_sample(ksum, q):
    n_seq, nb, nh, hd = ksum.shape
    return pl.pallas_call(
        _select_sample_kernel,
        grid=(n_seq,),
        in_specs=[pl.BlockSpec((None, nb, nh, hd), lambda b: (b, 0, 0, 0)),
                  pl.BlockSpec((None, nh, hd), lambda b: (b, 0, 0))],
        out_specs=pl.BlockSpec((None, TOP_K, nh, hd), lambda b: (b, 0, 0, 0)),
        out_shape=jax.ShapeDtypeStruct((n_seq, TOP_K, nh, hd), jnp.int32),
        compiler_params=_params("parallel"),
        name="select_sample",
    )(ksum, q.reshape(n_seq, nh, hd))


PAGES_PER_HEAD = TOP_K * (BLOCK // PAGE_SIZE)


def _attn_sample_kernel(pt_ref, sel_ref, q_ref, kn_ref, vn_ref, ck_ref, cv_ref, o_ref,
                        kbuf, vbuf, sem, *, scale, layer, n_pages):
    b = pl.program_id(0)
    n_seq = pl.num_programs(0)
    nh, hd = q_ref.shape
    ppb = BLOCK // PAGE_SIZE
    c = scale * LOG2E

    def copies(seq, slot):
        out = []
        for h in range(nh):
            for s in range(TOP_K):
                blk = sel_ref[(seq * TOP_K + s) * nh + h]
                for r in range(ppb):
                    page = pt_ref[seq * n_pages + blk * ppb + r]
                    i = h * PAGES_PER_HEAD + s * ppb + r
                    out.append(pltpu.make_async_copy(ck_ref.at[layer, page, :, h, :],
                                                     kbuf.at[slot, i], sem.at[slot, 0, i]))
                    out.append(pltpu.make_async_copy(cv_ref.at[layer, page, :, h, :],
                                                     vbuf.at[slot, i], sem.at[slot, 1, i]))
        return out

    slot = b % 2

    @pl.when(b == 0)
    def _():
        for cp in copies(b, slot):
            cp.start()

    @pl.when(b + 1 < n_seq)
    def _():
        for cp in copies(b + 1, 1 - slot):
            cp.start()

    for cp in copies(b, slot):
        cp.wait()

    n_keys = PAGES_PER_HEAD * PAGE_SIZE
    heads = range(nh)
    pages = [pl.ds(h * PAGES_PER_HEAD, PAGES_PER_HEAD) for h in heads]
    qb = [jnp.broadcast_to(q_ref[h:h + 1, :], (8, hd)).astype(BF16) for h in heads]
    logits = [lax.dot_general(qb[h], kbuf[slot, pages[h]].reshape(n_keys, hd).astype(BF16),
                              (((1,), (1,)), ((), ())), preferred_element_type=F32)
              for h in heads]
    own = [jnp.sum(qb[h].astype(F32) * kn_ref[h:h + 1, :].astype(BF16).astype(F32),
                   axis=-1, keepdims=True) for h in heads]
    m = [jnp.maximum(own[h], jnp.max(logits[h], axis=-1, keepdims=True)) for h in heads]
    p_own = [jnp.exp2((own[h] - m[h]) * c) for h in heads]
    p = [jnp.exp2((logits[h] - m[h]) * c) for h in heads]
    l = [p_own[h] + jnp.sum(p[h], axis=-1, keepdims=True) for h in heads]
    pv = [jnp.dot(p[h].astype(BF16), vbuf[slot, pages[h]].reshape(n_keys, hd).astype(BF16),
                  preferred_element_type=F32) for h in heads]
    for h in heads:
        acc = p_own[h].astype(BF16).astype(F32) * vn_ref[h:h + 1, :].astype(BF16).astype(F32)
        o_ref[h:h + 1, :] = ((acc + pv[h]) / l[h])[0:1, :]


def _attn_sample(page_table_flat, sel_flat, q, kn, vn, cache_k, cache_v, layer, n_pages):
    n_seq, d = q.shape
    hd = d // N_HEADS
    vec = pl.BlockSpec((None, N_HEADS, hd), lambda b, pt, sel: (b, 0, 0))
    to3 = lambda a: a.reshape(n_seq, N_HEADS, hd)
    n_buf = N_HEADS * PAGES_PER_HEAD
    out = pl.pallas_call(
        functools.partial(_attn_sample_kernel, scale=hd ** -0.5, layer=layer, n_pages=n_pages),
        grid_spec=pltpu.PrefetchScalarGridSpec(
            num_scalar_prefetch=2,
            grid=(n_seq,),
            in_specs=[vec, vec, vec, pl.BlockSpec(memory_space=pl.ANY),
                      pl.BlockSpec(memory_space=pl.ANY)],
            out_specs=vec,
            scratch_shapes=[pltpu.VMEM((2, n_buf, PAGE_SIZE, hd), F32),
                            pltpu.VMEM((2, n_buf, PAGE_SIZE, hd), F32),
                            pltpu.SemaphoreType.DMA((2, 2, n_buf))],
        ),
        out_shape=jax.ShapeDtypeStruct((n_seq, N_HEADS, hd), F32),
        compiler_params=_params("arbitrary"),
        name="attn_sample",
    )(page_table_flat, sel_flat, to3(q), to3(kn), to3(vn), cache_k, cache_v)
    return out.reshape(n_seq, d).astype(BF16)


def kernel(x_prompt, x_sample, cache_k, cache_v, state_conv, page_table, norm_mix, norm_ffn,
           norm_final, w_qkv, w_o, w_conv_in, conv_w, w_conv_out, w_up, w_down):
    bp, sp, d = x_prompt.shape
    bs, ss, _ = x_sample.shape
    depth = norm_mix.shape[0]
    hd = d // N_HEADS
    n_pages = page_table.shape[1]
    assert bp == 1 and ss == 1 and sp % BLOCK == 0 and d == N_HEADS * hd
    assert n_pages * PAGE_SIZE % BLOCK == 0
    assert cache_k.shape[2:] == (PAGE_SIZE, N_HEADS, hd) and CONV_W == 3

    xp = x_prompt.reshape(sp, d)
    xs = x_sample.reshape(bs, d)
    ck, cv = cache_k, cache_v
    pt_flat = page_table.reshape(-1)
    g_final = norm_final.reshape(1, d)
    bf = lambda w: w.astype(BF16)

    kp_l, vp_l, ks_l, vs_l, cp_l, cs_l = [], [], [], [], [], []
    for i in range(depth):
        g_mix = norm_mix[i].reshape(1, d)
        g_ffn = norm_ffn[i].reshape(1, d)
        last = i == depth - 1
        if i % 2 == 0:
            a = i // 2
            wqkv = bf(w_qkv[a])
            kp, vp, qt, kb, va, km = _qkv_prompt(xp, g_mix, wqkv)
            ap, ksum = _attn_prompt(qt, kb, va, km.reshape(sp // BLOCK, d), pt_flat, ck, a,
                                    bs, n_pages)
            qkv_s = _norm_proj(xs, g_mix, wqkv)
            qs, kn, vn = qkv_s[:, :d], qkv_s[:, d:2 * d], qkv_s[:, 2 * d:]
            sel = _select_sample(ksum, qs)[:, :, :, 0].reshape(-1)
            as_ = _attn_sample(pt_flat, sel, qs, kn, vn, ck, cv, a, n_pages)
            w_out = bf(w_o[a])
            kp_l.append(kp.reshape(bp, sp, N_HEADS, hd))
            vp_l.append(vp.reshape(bp, sp, N_HEADS, hd))
            ks_l.append(kn.reshape(bs, ss, N_HEADS, hd))
            vs_l.append(vn.reshape(bs, ss, N_HEADS, hd))
        else:
            c = i // 2
            win = bf(w_conv_in[c])
            zero_state = jnp.zeros((CONV_W - 1, d), F32)
            ap, cp = _conv_prompt(xp, g_mix, win, conv_w[c], zero_state)
            s0, s1 = state_conv[c, :, 0, :], state_conv[c, :, 1, :]
            as_, zs = _conv_sample(xs, g_mix, win, conv_w[c], s0, s1)
            w_out = bf(w_conv_out[c])
            cp_l.append(cp.reshape(bp, CONV_W - 1, d))
            cs_l.append(jnp.stack([s1, zs], axis=1))
        wup, wdn = bf(w_up[i]), bf(w_down[i])
        xp, xs = _post(ap, xp, as_, xs, w_out, g_ffn, wup, wdn, g_final, last)
    return (xp.reshape(bp, sp, d), xs.reshape(bs, ss, d), jnp.stack(kp_l), jnp.stack(vp_l),
            jnp.stack(ks_l), jnp.stack(vs_l), jnp.stack(cp_l), jnp.stack(cs_l))
```

```python
import functools

import jax
import jax.numpy as jnp
from jax import lax
from jax.experimental import pallas as pl
from jax.experimental.pallas import tpu as pltpu

F32 = jnp.float32
BF16 = jnp.bfloat16

N_HEADS = 8
BLOCK = 256
TOP_K = 3
PAGE_SIZE = 128
CONV_W = 3
EPS = 1e-6
LOG2E = 1.4426950408889634
GROUP = 4
Q_BLOCKS = 2
ONES_ROWS = 16

VMEM_LIMIT = 56 * 1024 * 1024


def _params(*sem):
    return pltpu.CompilerParams(dimension_semantics=sem, vmem_limit_bytes=VMEM_LIMIT)


def _rmsnorm(x, g):
    y = x * lax.rsqrt(jnp.mean(x * x, axis=-1, keepdims=True) + EPS)
    return y * g


def _top_k_rows(gate, n_valid):
    r = gate.shape[0]
    row = lax.broadcasted_iota(jnp.int32, gate.shape, 0).astype(F32)
    valid = row < n_valid
    g = jnp.where(valid, gate, -jnp.inf)
    sel = jnp.zeros(gate.shape, jnp.bool_)
    picks = []
    for _ in range(TOP_K):
        m = jnp.max(g, axis=0, keepdims=True)
        idx = jnp.min(jnp.where(g == m, row, float(r)), axis=0, keepdims=True)
        hit = row == idx
        sel = sel | hit
        g = jnp.where(hit, -jnp.inf, g)
        picks.append(idx)
    return sel & valid, jnp.concatenate(picks, axis=0)


def _qkv_prompt_kernel(x_ref, g_ref, w_ref, k_ref, v_ref, qt_ref, kb_ref, vt_ref, km_ref):
    d = x_ref.shape[1]
    tm = x_ref.shape[0]
    h = _rmsnorm(x_ref[...], g_ref[...]).astype(BF16)
    q = jnp.dot(h, w_ref[:, 0:d], preferred_element_type=F32)
    qt_ref[...] = q.T.astype(BF16)
    k = jnp.dot(h, w_ref[:, d:2 * d], preferred_element_type=F32)
    k_ref[...] = k
    kb_ref[...] = k.astype(BF16)
    km_ref[...] = jnp.sum(k.reshape(tm // BLOCK, BLOCK, d), axis=1) / BLOCK
    v = jnp.dot(h, w_ref[:, 2 * d:3 * d], preferred_element_type=F32)
    v_ref[...] = v
    vt = v.T.astype(BF16)
    hd = d // N_HEADS
    for hh in range(N_HEADS):
        vt_ref[hh, 0:hd, :] = vt[hh * hd:(hh + 1) * hd, :]
        vt_ref[hh, hd:, :] = jnp.ones((ONES_ROWS, tm), BF16)


def _qkv_prompt(x, g, w_bf, tm=512):
    s, d = x.shape
    nt = s // tm
    hd = d // N_HEADS
    row = pl.BlockSpec((tm, d), lambda i: (i, 0))
    col = pl.BlockSpec((d, tm), lambda i: (0, i))
    return pl.pallas_call(
        _qkv_prompt_kernel,
        grid=(nt,),
        in_specs=[row, pl.BlockSpec((1, d), lambda i: (0, 0)),
                  pl.BlockSpec((d, 3 * d), lambda i: (0, 0))],
        out_specs=[row, row, col, row,
                   pl.BlockSpec((N_HEADS, hd + ONES_ROWS, tm), lambda i: (0, 0, i)),
                   pl.BlockSpec((None, tm // BLOCK, d), lambda i: (i, 0, 0))],
        out_shape=[jax.ShapeDtypeStruct((s, d), F32), jax.ShapeDtypeStruct((s, d), F32),
                   jax.ShapeDtypeStruct((d, s), BF16), jax.ShapeDtypeStruct((s, d), BF16),
                   jax.ShapeDtypeStruct((N_HEADS, hd + ONES_ROWS, s), BF16),
                   jax.ShapeDtypeStruct((nt, tm // BLOCK, d), F32)],
        compiler_params=_params("parallel"),
        name="qkv_prompt",
    )(x, g, w_bf)


def _proj_kernel(x_ref, g_ref, w_ref, o_ref):
    h = _rmsnorm(x_ref[...], g_ref[...]).astype(BF16)
    o_ref[...] = jnp.dot(h, w_ref[...], preferred_element_type=F32)


def _norm_proj(x, g, w_bf, tn=1024):
    m, d = x.shape
    n = w_bf.shape[1]
    return pl.pallas_call(
        _proj_kernel,
        grid=(n // tn,),
        in_specs=[pl.BlockSpec((m, d), lambda j: (0, 0)), pl.BlockSpec((1, d), lambda j: (0, 0)),
                  pl.BlockSpec((d, tn), lambda j: (0, j))],
        out_specs=pl.BlockSpec((m, tn), lambda j: (0, j)),
        out_shape=jax.ShapeDtypeStruct((m, n), F32),
        compiler_params=_params("parallel"),
        name="norm_proj_sample",
    )(x, g, w_bf)


def _attn_prompt_kernel(pt_ref, qt_ref, k_ref, va_ref, km_ref, cache_ref, o_ref, ksum_ref,
                        bias_ref, acc_ref, sa_ref, sb_ref, pa_ref, pb_ref, pg_ref, pg_sem,
                        *, scale, layer):
    step = pl.program_id(0) * pl.num_programs(1) + pl.program_id(1)
    n_steps = pl.num_programs(0) * pl.num_programs(1)
    pps = pg_ref.shape[1]
    ppb = BLOCK // PAGE_SIZE
    slot = step % 2

    def page_copies(st, sl):
        return [pltpu.make_async_copy(cache_ref.at[layer, pt_ref[st * pps + r]],
                                      pg_ref.at[sl, r], pg_sem.at[sl, r]) for r in range(pps)]

    @pl.when(step == 0)
    def _():
        for cp in page_copies(step, slot):
            cp.start()

    ahead = jnp.minimum(step + 1, n_steps - 1)
    for cp in page_copies(ahead, 1 - slot):
        cp.start()
    for cp in page_copies(step, slot):
        cp.wait()
    for b in range(pps // ppb):
        acc = jnp.sum(pg_ref[slot, b * ppb], axis=0)
        for r in range(1, ppb):
            acc = acc + jnp.sum(pg_ref[slot, b * ppb + r], axis=0)
        ksum_ref[b] = acc

    nb = km_ref.shape[0]
    hd, tq = qt_ref.shape
    b0 = pl.program_id(1) * Q_BLOCKS
    gk = GROUP * BLOCK
    n_groups = nb // GROUP
    c = scale * LOG2E
    qtb = qt_ref[...]
    km = km_ref[...]
    km_hi = km.astype(BF16)
    km_lo = (km - km_hi.astype(F32)).astype(BF16)
    gate = (jnp.dot(km_hi, qtb, preferred_element_type=F32)
            + jnp.dot(km_lo, qtb, preferred_element_type=F32))
    qcol = lax.broadcasted_iota(jnp.int32, (1, tq), 1)
    own = b0 + qcol // BLOCK
    sel, _ = _top_k_rows(gate, own.astype(F32))
    picked = jnp.where(sel, 0.0, -jnp.inf)
    blk = lax.broadcasted_iota(jnp.int32, (nb, tq), 0)
    bias_ref[...] = jnp.where(blk < b0, picked, -jnp.inf)

    def scores(g, s_ref):
        off = pl.multiple_of(g * gk, gk)
        s_ref[...] = jnp.dot(k_ref[pl.ds(off, gk), :], qtb, preferred_element_type=F32)
        return jnp.concatenate(
            [jnp.max(s_ref[b * BLOCK:(b + 1) * BLOCK, :], axis=0, keepdims=True)
             for b in range(GROUP)], axis=0)

    def weights(g, s_ref, cm, p_ref, m):
        bias = [bias_ref[pl.ds(g * GROUP + b, 1), :] for b in range(GROUP)]
        m_new = m
        for b in range(GROUP):
            m_new = jnp.maximum(m_new, cm[b:b + 1, :] + bias[b])
        for b in range(GROUP):
            shift = jnp.where(bias[b] == 0.0, m_new, jnp.inf)
            rows = slice(b * BLOCK, (b + 1) * BLOCK)
            p_ref[rows, :] = jnp.exp2((s_ref[rows, :] - shift) * c).astype(BF16)
        return m_new, jnp.exp2((m - m_new) * c)

    def values(g, p_ref, alpha):
        off = pl.multiple_of(g * gk, gk)
        acc_ref[...] = alpha * acc_ref[...] + jnp.dot(
            va_ref[:, pl.ds(off, gk)], p_ref[...], preferred_element_type=F32)

    off_own = pl.multiple_of(b0 * BLOCK, tq)
    s_own = jnp.dot(k_ref[pl.ds(off_own, tq), :], qtb, preferred_element_type=F32)
    key = lax.broadcasted_iota(jnp.int32, s_own.shape, 0)
    qry = lax.broadcasted_iota(jnp.int32, s_own.shape, 1)
    picked_first = jnp.max(jnp.where(blk == b0, picked, -jnp.inf), axis=0, keepdims=True)
    first_ok = jnp.where(qcol < BLOCK, 0.0, picked_first)
    s_own = jnp.where(key <= qry, s_own, -jnp.inf) + jnp.where(key < BLOCK, first_ok, 0.0)
    m0 = jnp.max(s_own, axis=0, keepdims=True)
    p_own = jnp.exp2((s_own - m0) * c).astype(BF16)
    acc_ref[...] = jnp.dot(va_ref[:, pl.ds(off_own, tq)], p_own, preferred_element_type=F32)

    pb_ref[...] = jnp.zeros_like(pb_ref)
    n_pairs = (b0 + 2 * GROUP - 1) // (2 * GROUP)
    cm_a0 = scores(0, sa_ref)

    def pair(t, carry):
        m, alpha_b, cm_a = carry
        g0 = 2 * t
        cm_b = scores(g0 + 1, sb_ref)
        m, alpha_a = weights(g0, sa_ref, cm_a, pa_ref, m)
        values(jnp.maximum(g0 - 1, 0), pb_ref, alpha_b)
        cm_a = scores(jnp.minimum(g0 + 2, n_groups - 1), sa_ref)
        m, alpha_b = weights(g0 + 1, sb_ref, cm_b, pb_ref, m)
        values(g0, pa_ref, alpha_a)
        return m, alpha_b, cm_a

    m, alpha_b, _ = lax.fori_loop(0, n_pairs, pair, (m0, jnp.ones((1, tq), F32), cm_a0))
    values(jnp.maximum(2 * n_pairs - 1, 0), pb_ref, alpha_b)
    acc = acc_ref[...]
    o_ref[...] = (acc[0:hd, :] / acc[hd:hd + 1, :]).T.astype(o_ref.dtype)

    @pl.when(step == n_steps - 1)
    def _():
        for cp in page_copies(ahead, 1 - slot):
            cp.wait()


def _attn_prompt(qt, kb, va, km, page_table_flat, cache, layer, n_seq, n_pages):
    d, s = qt.shape
    hd = d // N_HEADS
    nb = s // BLOCK
    assert nb % (2 * GROUP) == 0 and Q_BLOCKS == 2
    gk = GROUP * BLOCK
    tq = Q_BLOCKS * BLOCK
    nt = nb // Q_BLOCKS
    steps = N_HEADS * nt
    ppb = BLOCK // PAGE_SIZE
    pps = n_seq * n_pages // steps
    assert pps * steps == n_seq * n_pages and pps % ppb == 0 and n_pages % pps == 0
    steps_per_seq = n_pages // pps

    ksum_spec = pl.BlockSpec(
        (None, pps // ppb, N_HEADS, hd),
        lambda h, i, pt: ((h * nt + i) // steps_per_seq, (h * nt + i) % steps_per_seq, 0, 0))
    return pl.pallas_call(
        functools.partial(_attn_prompt_kernel, scale=hd ** -0.5, layer=layer),
        grid_spec=pltpu.PrefetchScalarGridSpec(
            num_scalar_prefetch=1,
            grid=(N_HEADS, nt),
            in_specs=[pl.BlockSpec((hd, tq), lambda h, i, pt: (h, i)),
                      pl.BlockSpec((s, hd), lambda h, i, pt: (0, h)),
                      pl.BlockSpec((None, hd + ONES_ROWS, s), lambda h, i, pt: (h, 0, 0)),
                      pl.BlockSpec((nb, hd), lambda h, i, pt: (0, h)),
                      pl.BlockSpec(memory_space=pl.ANY)],
            out_specs=[pl.BlockSpec((tq, hd), lambda h, i, pt: (i, h)), ksum_spec],
            scratch_shapes=[pltpu.VMEM((nb, tq), F32),
                            pltpu.VMEM((hd + ONES_ROWS, tq), F32),
                            pltpu.VMEM((gk, tq), F32), pltpu.VMEM((gk, tq), F32),
                            pltpu.VMEM((gk, tq), BF16), pltpu.VMEM((gk, tq), BF16),
                            pltpu.VMEM((2, pps, PAGE_SIZE, N_HEADS, hd), F32),
                            pltpu.SemaphoreType.DMA((2, pps))],
        ),
        out_shape=[jax.ShapeDtypeStruct((s, d), BF16),
                   jax.ShapeDtypeStruct((n_seq, n_pages // ppb, N_HEADS, hd), F32)],
        compiler_params=_params("arbitrary", "arbitrary"),
        name="attn_prompt",
    )(page_table_flat, qt, kb, va, km, cache)


def _post_kernel(a_ref, x_ref, as_ref, xs_ref, wo_ref, g_ref, wup_ref, wdn_ref, gf_ref,
                 out_ref, outs_ref, *, final_norm, tf):
    ff = wup_ref.shape[1]

    def rows(a, x):
        y = x + jnp.dot(a, wo_ref[...], preferred_element_type=F32)
        h = _rmsnorm(y, g_ref[...]).astype(BF16)
        for f in range(ff // tf):
            u = jnp.dot(h, wup_ref[:, f * tf:(f + 1) * tf], preferred_element_type=F32)
            u = jnp.maximum(u, 0.0)
            y = y + jnp.dot((u * u).astype(BF16), wdn_ref[f * tf:(f + 1) * tf, :],
                            preferred_element_type=F32)
        if final_norm:
            y = _rmsnorm(y, gf_ref[...])
        return y

    out_ref[...] = rows(a_ref[...], x_ref[...])

    @pl.when(pl.program_id(0) == pl.num_programs(0) - 1)
    def _():
        outs_ref[...] = rows(as_ref[...], xs_ref[...])


def _post(a_bf, x, as_bf, xs, wo_bf, g_ffn, wup_bf, wdn_bf, g_final, final_norm, tm=512,
          tf=1024):
    s, d = x.shape
    ms = xs.shape[0]
    ff = wup_bf.shape[1]
    row = lambda i: (i, 0)
    fix = lambda i: (0, 0)
    once = pl.Buffered(1)
    return pl.pallas_call(
        functools.partial(_post_kernel, final_norm=final_norm, tf=tf),
        grid=(s // tm,),
        in_specs=[pl.BlockSpec((tm, d), row), pl.BlockSpec((tm, d), row),
                  pl.BlockSpec((ms, d), fix), pl.BlockSpec((ms, d), fix),
                  pl.BlockSpec((d, d), fix, pipeline_mode=once), pl.BlockSpec((1, d), fix),
                  pl.BlockSpec((d, ff), fix, pipeline_mode=once),
                  pl.BlockSpec((ff, d), fix, pipeline_mode=once),
                  pl.BlockSpec((1, d), fix)],
        out_specs=[pl.BlockSpec((tm, d), row), pl.BlockSpec((ms, d), fix)],
        out_shape=[jax.ShapeDtypeStruct((s, d), F32), jax.ShapeDtypeStruct((ms, d), F32)],
        compiler_params=_params("arbitrary"),
        name="post_mlp",
    )(a_bf, x, as_bf, xs, wo_bf, g_ffn, wup_bf, wdn_bf, g_final)


def _conv_prompt_kernel(x_ref, g_ref, w_ref, cw_ref, st_ref, a_ref, so_ref, carry_ref):
    tm, d = x_ref.shape

    @pl.when(pl.program_id(0) == 0)
    def _():
        carry_ref[...] = st_ref[...]

    h = _rmsnorm(x_ref[...], g_ref[...]).astype(BF16)
    gc = jnp.dot(h, w_ref[:, d:2 * d], preferred_element_type=F32)
    u = jnp.dot(h, w_ref[:, 2 * d:3 * d], preferred_element_type=F32)
    z = gc * u
    prev2 = carry_ref[0:1, :]
    prev1 = carry_ref[1:2, :]
    row = lax.broadcasted_iota(jnp.int32, (tm, d), 0)
    z1 = jnp.where(row == 0, prev1, pltpu.roll(z, 1, 0))
    z2 = jnp.where(row == 0, prev2, jnp.where(row == 1, prev1, pltpu.roll(z, 2, 0)))
    y = cw_ref[0:1, :] * z2
    y = y + cw_ref[1:2, :] * z1
    y = y + cw_ref[2:3, :] * z
    gb = jnp.dot(h, w_ref[:, 0:d], preferred_element_type=F32)
    a_ref[...] = (gb * y).astype(a_ref.dtype)
    tail = z[tm - (CONV_W - 1):tm, :]
    carry_ref[...] = tail
    so_ref[...] = tail


def _conv_prompt(x, g, w_bf, cw, state, tm=512):
    s, d = x.shape
    fix = lambda i: (0, 0)
    return pl.pallas_call(
        _conv_prompt_kernel,
        grid=(s // tm,),
        in_specs=[pl.BlockSpec((tm, d), lambda i: (i, 0)), pl.BlockSpec((1, d), fix),
                  pl.BlockSpec((d, 3 * d), fix), pl.BlockSpec((CONV_W, d), fix),
                  pl.BlockSpec((CONV_W - 1, d), fix)],
        out_specs=[pl.BlockSpec((tm, d), lambda i: (i, 0)), pl.BlockSpec((CONV_W - 1, d), fix)],
        out_shape=[jax.ShapeDtypeStruct((s, d), BF16), jax.ShapeDtypeStruct((CONV_W - 1, d), F32)],
        scratch_shapes=[pltpu.VMEM((CONV_W - 1, d), F32)],
        compiler_params=_params("arbitrary"),
        name="conv_prompt",
    )(x, g, w_bf, cw, state)


def _conv_sample_kernel(x_ref, g_ref, w_ref, cw_ref, s0_ref, s1_ref, a_ref, z_ref):
    d = x_ref.shape[1]
    h = _rmsnorm(x_ref[...], g_ref[...]).astype(BF16)
    gb = jnp.dot(h, w_ref[:, 0:d], preferred_element_type=F32)
    gc = jnp.dot(h, w_ref[:, d:2 * d], preferred_element_type=F32)
    u = jnp.dot(h, w_ref[:, 2 * d:3 * d], preferred_element_type=F32)
    z = gc * u
    y = cw_ref[0:1, :] * s0_ref[...]
    y = y + cw_ref[1:2, :] * s1_ref[...]
    y = y + cw_ref[2:3, :] * z
    a_ref[...] = (gb * y).astype(a_ref.dtype)
    z_ref[...] = z


def _conv_sample(x, g, w_bf, cw, s0, s1):
    m, d = x.shape
    return pl.pallas_call(
        _conv_sample_kernel,
        out_shape=[jax.ShapeDtypeStruct((m, d), BF16), jax.ShapeDtypeStruct((m, d), F32)],
        compiler_params=pltpu.CompilerParams(vmem_limit_bytes=VMEM_LIMIT),
        name="conv_sample",
    )(x, g, w_bf, cw, s0, s1)


def _select_sample_kernel(ks_ref, q_ref, sel_ref):
    n_seq, nb = ks_ref.shape[:2]

    def one(b, carry):
        prod = (ks_ref[b] / BLOCK) * q_ref[b]
        gate = jnp.sum(prod, axis=-1, keepdims=True)
        _, picks = _top_k_rows(gate, float(nb))
        sel_ref[b] = jnp.broadcast_to(picks, sel_ref.shape[1:]).astype(jnp.int32)
        return carry

    lax.fori_loop(0, n_seq, one, 0)


def _select_sample(ksum, q):
    n_seq, nb, nh, hd = ksum.shape
    return pl.pallas_call(
        _select_sample_kernel,
        out_shape=jax.ShapeDtypeStruct((n_seq, TOP_K, nh, hd), jnp.int32),
        compiler_params=pltpu.CompilerParams(vmem_limit_bytes=VMEM_LIMIT),
        name="select_sample",
    )(ksum, q.reshape(n_seq, nh, hd))


PAGES_PER_HEAD = TOP_K * (BLOCK // PAGE_SIZE)


def _attn_sample_kernel(pt_ref, sel_ref, q_ref, kn_ref, vn_ref, ck_ref, cv_ref, o_ref,
                        kbuf, vbuf, sem, *, scale, layer, n_pages):
    b = pl.program_id(0)
    n_seq = pl.num_programs(0)
    nh, hd = q_ref.shape
    ppb = BLOCK // PAGE_SIZE
    c = scale * LOG2E

    def copies(seq, slot):
        out = []
        for h in range(nh):
            for s in range(TOP_K):
                blk = sel_ref[(seq * TOP_K + s) * nh + h]
                for r in range(ppb):
                    page = pt_ref[seq * n_pages + blk * ppb + r]
                    i = h * PAGES_PER_HEAD + s * ppb + r
                    out.append(pltpu.make_async_copy(ck_ref.at[layer, page, :, h, :],
                                                     kbuf.at[slot, i], sem.at[slot, 0, i]))
                    out.append(pltpu.make_async_copy(cv_ref.at[layer, page, :, h, :],
                                                     vbuf.at[slot, i], sem.at[slot, 1, i]))
        return out

    slot = b % 2

    @pl.when(b == 0)
    def _():
        for cp in copies(b, slot):
            cp.start()

    @pl.when(b + 1 < n_seq)
    def _():
        for cp in copies(b + 1, 1 - slot):
            cp.start()

    for cp in copies(b, slot):
        cp.wait()

    n_keys = PAGES_PER_HEAD * PAGE_SIZE
    heads = range(nh)
    pages = [pl.ds(h * PAGES_PER_HEAD, PAGES_PER_HEAD) for h in heads]
    qb = [jnp.broadcast_to(q_ref[h:h + 1, :], (8, hd)).astype(BF16) for h in heads]
    logits = [lax.dot_general(qb[h], kbuf[slot, pages[h]].reshape(n_keys, hd).astype(BF16),
                              (((1,), (1,)), ((), ())), preferred_element_type=F32)
              for h in heads]
    own = [jnp.sum(qb[h].astype(F32) * kn_ref[h:h + 1, :].astype(BF16).astype(F32),
                   axis=-1, keepdims=True) for h in heads]
    m = [jnp.maximum(own[h], jnp.max(logits[h], axis=-1, keepdims=True)) for h in heads]
    p_own = [jnp.exp2((own[h] - m[h]) * c) for h in heads]
    p = [jnp.exp2((logits[h] - m[h]) * c) for h in heads]
    l = [p_own[h] + jnp.sum(p[h], axis=-1, keepdims=True) for h in heads]
    pv = [jnp.dot(p[h].astype(BF16), vbuf[slot, pages[h]].reshape(n_keys, hd).astype(BF16),
                  preferred_element_type=F32) for h in heads]
    for h in heads:
        acc = p_own[h].astype(BF16).astype(F32) * vn_ref[h:h + 1, :].astype(BF16).astype(F32)
        o_ref[h:h + 1, :] = ((acc + pv[h]) / l[h])[0:1, :]


def _attn_sample(page_table_flat, sel_flat, q, kn, vn, cache_k, cache_v, layer, n_pages):
    n_seq, d = q.shape
    hd = d // N_HEADS
    vec = pl.BlockSpec((None, N_HEADS, hd), lambda b, pt, sel: (b, 0, 0))
    to3 = lambda a: a.reshape(n_seq, N_HEADS, hd)
    n_buf = N_HEADS * PAGES_PER_HEAD
    out = pl.pallas_call(
        functools.partial(_attn_sample_kernel, scale=hd ** -0.5, layer=layer, n_pages=n_pages),
        grid_spec=pltpu.PrefetchScalarGridSpec(
            num_scalar_prefetch=2,
            grid=(n_seq,),
            in_specs=[vec, vec, vec, pl.BlockSpec(memory_space=pl.ANY),
                      pl.BlockSpec(memory_space=pl.ANY)],
            out_specs=vec,
            scratch_shapes=[pltpu.VMEM((2, n_buf, PAGE_SIZE, hd), F32),
                            pltpu.VMEM((2, n_buf, PAGE_SIZE, hd), F32),
                            pltpu.SemaphoreType.DMA((2, 2, n_buf))],
        ),
        out_shape=jax.ShapeDtypeStruct((n_seq, N_HEADS, hd), F32),
        compiler_params=_params("arbitrary"),
        name="attn_sample",
    )(page_table_flat, sel_flat, to3(q), to3(kn), to3(vn), cache_k, cache_v)
    return out.reshape(n_seq, d).astype(BF16)


def kernel(x_prompt, x_sample, cache_k, cache_v, state_conv, page_table, norm_mix, norm_ffn,
           norm_final, w_qkv, w_o, w_conv_in, conv_w, w_conv_out, w_up, w_down):
    bp, sp, d = x_prompt.shape
    bs, ss, _ = x_sample.shape
    depth = norm_mix.shape[0]
    hd = d // N_HEADS
    n_pages = page_table.shape[1]
    assert bp == 1 and ss == 1 and sp % BLOCK == 0 and d == N_HEADS * hd
    assert n_pages * PAGE_SIZE % BLOCK == 0
    assert cache_k.shape[2:] == (PAGE_SIZE, N_HEADS, hd) and CONV_W == 3

    xp = x_prompt.reshape(sp, d)
    xs = x_sample.reshape(bs, d)
    ck, cv = cache_k, cache_v
    pt_flat = page_table.reshape(-1)
    g_final = norm_final.reshape(1, d)
    bf = lambda w: w.astype(BF16)

    kp_l, vp_l, ks_l, vs_l, cp_l, cs_l = [], [], [], [], [], []
    for i in range(depth):
        g_mix = norm_mix[i].reshape(1, d)
        g_ffn = norm_ffn[i].reshape(1, d)
        last = i == depth - 1
        if i % 2 == 0:
            a = i // 2
            wqkv = bf(w_qkv[a])
            kp, vp, qt, kb, va, km = _qkv_prompt(xp, g_mix, wqkv)
            ap, ksum = _attn_prompt(qt, kb, va, km.reshape(sp // BLOCK, d), pt_flat, ck, a,
                                    bs, n_pages)
            qkv_s = _norm_proj(xs, g_mix, wqkv)
            qs, kn, vn = qkv_s[:, :d], qkv_s[:, d:2 * d], qkv_s[:, 2 * d:]
            sel = _select_sample(ksum, qs)[:, :, :, 0].reshape(-1)
            as_ = _attn_sample(pt_flat, sel, qs, kn, vn, ck, cv, a, n_pages)
            w_out = bf(w_o[a])
            kp_l.append(kp.reshape(bp, sp, N_HEADS, hd))
            vp_l.append(vp.reshape(bp, sp, N_HEADS, hd))
            ks_l.append(kn.reshape(bs, ss, N_HEADS, hd))
            vs_l.append(vn.reshape(bs, ss, N_HEADS, hd))
        else:
            c = i // 2
            win = bf(w_conv_in[c])
            zero_state = jnp.zeros((CONV_W - 1, d), F32)
            ap, cp = _conv_prompt(xp, g_mix, win, conv_w[c], zero_state)
            s0, s1 = state_conv[c, :, 0, :], state_conv[c, :, 1, :]
            as_, zs = _conv_sample(xs, g_mix, win, conv_w[c], s0, s1)
            w_out = bf(w_conv_out[c])
            cp_l.append(cp.reshape(bp, CONV_W - 1, d))
            cs_l.append(jnp.stack([s1, zs], axis=1))
        wup, wdn = bf(w_up[i]), bf(w_down[i])
        xp, xs = _post(ap, xp, as_, xs, w_out, g_ffn, wup, wdn, g_final, last)
    return (xp.reshape(bp, sp, d), xs.reshape(bs, ss, d), jnp.stack(kp_l), jnp.stack(vp_l),
            jnp.stack(ks_l), jnp.stack(vs_l), jnp.stack(cp_l), jnp.stack(cs_l))
```
